```python
import math
import jax, jax.numpy as jnp
from jax import lax
import numpy as np

D_MODEL = 1024
BATCH = 8
SEQ = 4096
DEPTH = 1
DEC_BATCH = 1
DEC_SEQ = 16384
PAST_LEN = 128

N_META = 16
GRID_W = 64
SSD_EXPAND = 2
D_INNER = SSD_EXPAND * D_MODEL
SSD_HEADDIM = 64
SSD_HEADS = D_INNER // SSD_HEADDIM
SSD_GROUPS = 8
D_STATE = 128
D_CONV = 5
CHUNK = 128
D_XBC = D_INNER + 2 * SSD_GROUPS * D_STATE
NA_HEADS = 16
NA_HEADDIM = 64
D_NA = NA_HEADS * NA_HEADDIM
WIN_H = 8
WIN_W = 16
D_FF = 2816
FFN_CONV = 3
EPS = 1e-6
IN_SPLITS = [D_INNER, D_XBC, SSD_HEADS, SSD_HEADS, D_NA, D_NA, D_NA, D_MODEL, D_MODEL]
D_IN_PROJ = sum(IN_SPLITS)

kernel_name = "hybrid_ssd_natten_meta_encoder"


def rmsnorm(x, g):
    xf = x.astype(jnp.float32)
    y = xf * lax.rsqrt(jnp.mean(xf * xf, axis=-1, keepdims=True) + EPS)
    return (y * g.astype(jnp.float32)).astype(x.dtype)


def dwconv_centred(x, w, b):
    pad = (w.shape[0] - 1) // 2
    out = lax.conv_general_dilated(x, w[:, None, :].astype(x.dtype), window_strides=(1,),
                                   padding=[(pad, pad)], dimension_numbers=('NWC', 'WIO', 'NWC'),
                                   feature_group_count=x.shape[-1])
    return out + b.astype(x.dtype)


def segsum(a):
    cs = jnp.cumsum(a, axis=-1)
    l = a.shape[-1]
    diff = cs[..., :, None] - cs[..., None, :]
    return jnp.where(jnp.tril(jnp.ones((l, l), dtype=bool)), diff, -jnp.inf)


def ssd_chunked(x, dt, A, B, C):
    b, Lp, H, P = x.shape
    nc = Lp // CHUNK
    G = B.shape[2]
    R = H // G
    xdt = (x * dt[..., None]).reshape(b, nc, CHUNK, G, R, P)
    a = (dt * A).reshape(b, nc, CHUNK, G, R).transpose(0, 3, 4, 1, 2)
    Bc = B.reshape(b, nc, CHUNK, G, -1)
    Cc = C.reshape(b, nc, CHUNK, G, -1)
    a_cs = jnp.cumsum(a, axis=-1)
    CB = jnp.einsum('bclgn,bcsgn->bgcls', Cc, Bc)
    M = CB[:, :, None] * jnp.exp(segsum(a))
    y_diag = jnp.einsum('bgrcls,bcsgrp->bclgrp', M, xdt)
    decay_states = jnp.exp(a_cs[..., -1:] - a_cs)
    states = jnp.einsum('bclgn,bgrcl,bclgrp->cbgrpn', Bc, decay_states, xdt)
    chunk_decay = jnp.exp(a_cs[..., -1]).transpose(3, 0, 1, 2)

    def step(s, inp):
        st, dec = inp
        return s * dec[..., None, None] + st, s

    s0 = jnp.zeros(states.shape[1:], states.dtype)
    _, prev = lax.scan(step, s0, (states, chunk_decay))
    y_off = jnp.einsum('bclgn,cbgrpn,bgrcl->bclgrp', Cc, prev, jnp.exp(a_cs))
    return (y_diag + y_off).reshape(b, Lp, H, P)


def ssd_branch(z, xbc, dt_f_raw, dt_b_raw, conv_w, conv_b, dt_bias_f, dt_bias_b,
               a_log_f, a_log_b, d_skip, norm_g):
    bsz, L, _ = xbc.shape
    f32 = jnp.float32
    xbc = jax.nn.silu(dwconv_centred(xbc, conv_w, conv_b)).astype(f32)
    xs, Bm, Cm = jnp.split(xbc, [D_INNER, D_INNER + SSD_GROUPS * D_STATE], axis=-1)
    x = xs.reshape(bsz, L, SSD_HEADS, SSD_HEADDIM)
    Bm = Bm.reshape(bsz, L, SSD_GROUPS, D_STATE)
    Cm = Cm.reshape(bsz, L, SSD_GROUPS, D_STATE)
    pad = CHUNK - N_META

    def padseq(t):
        return jnp.pad(t, [(0, 0), (pad, 0)] + [(0, 0)] * (t.ndim - 2))

    xp, Bp, Cp = padseq(x), padseq(Bm), padseq(Cm)

    def direction(dt_raw, dt_bias, a_log, flip):
        dtp = padseq(jax.nn.softplus(dt_raw.astype(f32) + dt_bias.astype(f32)))
        A = -jnp.exp(a_log.astype(f32))
        args = (xp, dtp, Bp, Cp)
        if flip:
            args = tuple(jnp.flip(t, axis=1) for t in args)
        y = ssd_chunked(args[0], args[1], A, args[2], args[3])
        return jnp.flip(y, axis=1) if flip else y

    y = direction(dt_f_raw, dt_bias_f, a_log_f, False) + direction(dt_b_raw, dt_bias_b, a_log_b, True)
    y = y[:, pad:] + d_skip.astype(f32)[:, None] * x
    y = y.reshape(bsz, L, D_INNER) * jax.nn.silu(z.astype(f32))
    yg = y.reshape(bsz, L, SSD_GROUPS, D_INNER // SSD_GROUPS)
    yg = yg * lax.rsqrt(jnp.mean(yg * yg, axis=-1, keepdims=True) + EPS)
    y = yg.reshape(bsz, L, D_INNER) * norm_g.astype(f32)
    return y.astype(z.dtype)


def na_branch(q, k, v, q_g, k_g, rel_bias, meta_bias):
    bsz, L, _ = q.shape
    T = L - N_META
    rows = T // GRID_W
    kh = min(WIN_H, rows)
    f32 = jnp.float32
    q = rmsnorm(q.reshape(bsz, L, NA_HEADS, NA_HEADDIM), q_g) * (NA_HEADDIM ** -0.5)
    k = rmsnorm(k.reshape(bsz, L, NA_HEADS, NA_HEADDIM), k_g)
    v = v.reshape(bsz, L, NA_HEADS, NA_HEADDIM)
    qm, km, vm = q[:, :N_META], k[:, :N_META], v[:, :N_META]
    qg = q[:, N_META:].reshape(bsz, rows, GRID_W, NA_HEADS, NA_HEADDIM)
    kg = k[:, N_META:].reshape(bsz, rows, GRID_W, NA_HEADS, NA_HEADDIM)
    vg = v[:, N_META:].reshape(bsz, rows, GRID_W, NA_HEADS, NA_HEADDIM)
    mb = meta_bias.astype(f32)[None, :, None, :]
    s_mm = jnp.einsum('bqhd,bmhd->bhqm', qm, km).astype(f32) + mb
    o_meta = jnp.einsum('bhqm,bmhd->bqhd', jax.nn.softmax(s_mm, axis=-1).astype(v.dtype), vm)
    cols = np.arange(GRID_W)
    cstart = np.clip(cols - WIN_W // 2, 0, GRID_W - WIN_W)
    col_idx = cstart[:, None] + np.arange(WIN_W)[None, :]
    col_off = col_idx - cols[:, None] + (WIN_W - 1)

    def row_block(r):
        rstart = jnp.clip(r - kh // 2, 0, rows - kh)
        kb = lax.dynamic_slice_in_dim(kg, rstart, kh, axis=1)
        vb = lax.dynamic_slice_in_dim(vg, rstart, kh, axis=1)
        kw = kb[:, :, col_idx]
        vw = vb[:, :, col_idx]
        qr = lax.dynamic_index_in_dim(qg, r, axis=1, keepdims=False)
        row_off = rstart + jnp.arange(kh) - r + (WIN_H - 1)
        bias = rel_bias[:, row_off[:, None, None], col_off[None]]
        s_win = jnp.einsum('bwhd,biwjhd->bhwij', qr, kw).astype(f32) \
            + bias.astype(f32).transpose(0, 2, 1, 3)[None]
        s_met = jnp.einsum('bwhd,bmhd->bhwm', qr, km).astype(f32) + mb
        s = jnp.concatenate([s_met, s_win.reshape(bsz, NA_HEADS, GRID_W, kh * WIN_W)], axis=-1)
        p = jax.nn.softmax(s, axis=-1).astype(v.dtype)
        pm = p[..., :N_META]
        pw = p[..., N_META:].reshape(bsz, NA_HEADS, GRID_W, kh, WIN_W)
        return jnp.einsum('bhwm,bmhd->bwhd', pm, vm) + jnp.einsum('bhwij,biwjhd->bwhd', pw, vw)

    o_grid = lax.map(row_block, jnp.arange(rows))
    o_grid = o_grid.transpose(1, 0, 2, 3, 4).reshape(bsz, T, D_NA)
    return jnp.concatenate([o_meta.reshape(bsz, N_META, D_NA), o_grid], axis=1)


def hybrid_layer(h, g_mix, w_in, ssd_conv_w, ssd_conv_b, dt_bias_f, dt_bias_b, a_log_f, a_log_b,
                 d_skip, ssd_norm_g, q_norm_g, k_norm_g, rel_bias, meta_bias, w_br_ssd, w_br_na,
                 w_out, g_ffn, w_up, ffn_conv_w, ffn_conv_b, w_down):
    u = rmsnorm(h, g_mix)
    proj = u @ w_in
    z, xbc, dtf, dtb, q, k, v, gs, gn = jnp.split(proj, np.cumsum(IN_SPLITS)[:-1], axis=-1)
    y_ssd = ssd_branch(z, xbc, dtf, dtb, ssd_conv_w, ssd_conv_b, dt_bias_f, dt_bias_b,
                       a_log_f, a_log_b, d_skip, ssd_norm_g)
    y_na = na_branch(q, k, v, q_norm_g, k_norm_g, rel_bias, meta_bias)
    merged = jax.nn.sigmoid(gs) * (y_ssd @ w_br_ssd) + jax.nn.sigmoid(gn) * (y_na @ w_br_na)
    h = h + merged @ w_out
    u = rmsnorm(h, g_ffn)
    up = dwconv_centred(u @ w_up, ffn_conv_w, ffn_conv_b)
    a, g = jnp.split(up, 2, axis=-1)
    return h + (jax.nn.silu(g) * a) @ w_down


def setup_inputs(seed: int = 0) -> dict:
    key = jax.random.key(seed)
    ks = jax.random.split(key, 32)
    nrm = jax.random.normal
    f32 = jnp.float32
    dt0 = jnp.exp(jax.random.uniform(ks[6], (2, DEPTH, SSD_HEADS), f32, math.log(1e-3), math.log(1e-1)))
    dt_bias = dt0 + jnp.log(-jnp.expm1(-dt0))
    a_log = jnp.log(jax.random.uniform(ks[7], (2, DEPTH, SSD_HEADS), f32, 1.0, 16.0))
    return {
        "x_prompt": nrm(ks[0], (BATCH, SEQ, D_MODEL), f32),
        "x_sample": nrm(ks[1], (DEC_BATCH, DEC_SEQ, D_MODEL), f32),
        "meta_tokens": nrm(ks[2], (N_META, D_MODEL), f32),
        "g_mix": 1.0 + 0.01 * nrm(ks[3], (DEPTH, D_MODEL), f32),
        "w_in": nrm(ks[4], (DEPTH, D_MODEL, D_IN_PROJ), f32) * D_MODEL ** -0.5,
        "ssd_conv_w": nrm(ks[5], (DEPTH, D_CONV, D_XBC), f32) * D_CONV ** -0.5,
        "ssd_conv_b": 0.01 * nrm(ks[8], (DEPTH, D_XBC), f32),
        "dt_bias_f": dt_bias[0],
        "dt_bias_b": dt_bias[1],
        "a_log_f": a_log[0],
        "a_log_b": a_log[1],
        "d_skip": 1.0 + 0.1 * nrm(ks[9], (DEPTH, SSD_HEADS), f32),
        "ssd_norm_g": 1.0 + 0.01 * nrm(ks[10], (DEPTH, D_INNER), f32),
        "q_norm_g": 1.0 + 0.01 * nrm(ks[11], (DEPTH, NA_HEADDIM), f32),
        "k_norm_g": 1.0 + 0.01 * nrm(ks[12], (DEPTH, NA_HEADDIM), f32),
        "rel_bias": 0.1 * nrm(ks[13], (DEPTH, NA_HEADS, 2 * WIN_H - 1, 2 * WIN_W - 1), f32),
        "meta_bias": 0.1 * nrm(ks[14], (DEPTH, NA_HEADS, N_META), f32),
        "w_br_ssd": nrm(ks[15], (DEPTH, D_INNER, D_MODEL), f32) * D_INNER ** -0.5,
        "w_br_na": nrm(ks[16], (DEPTH, D_NA, D_MODEL), f32) * D_NA ** -0.5,
        "w_out": nrm(ks[17], (DEPTH, D_MODEL, D_MODEL), f32) * D_MODEL ** -0.5,
        "g_ffn": 1.0 + 0.01 * nrm(ks[18], (DEPTH, D_MODEL), f32),
        "w_up": nrm(ks[19], (DEPTH, D_MODEL, 2 * D_FF), f32) * D_MODEL ** -0.5,
        "ffn_conv_w": nrm(ks[20], (DEPTH, FFN_CONV, 2 * D_FF), f32) * FFN_CONV ** -0.5,
        "ffn_conv_b": 0.01 * nrm(ks[21], (DEPTH, 2 * D_FF), f32),
        "w_down": nrm(ks[22], (DEPTH, D_FF, D_MODEL), f32) * D_FF ** -0.5,
    }


def reference(x_prompt, x_sample, meta_tokens, g_mix, w_in, ssd_conv_w, ssd_conv_b, dt_bias_f,
              dt_bias_b, a_log_f, a_log_b, d_skip, ssd_norm_g, q_norm_g, k_norm_g, rel_bias,
              meta_bias, w_br_ssd, w_br_na, w_out, g_ffn, w_up, ffn_conv_w, ffn_conv_b, w_down):
    def run(x):
        bsz = x.shape[0]
        meta = jnp.broadcast_to(meta_tokens.astype(x.dtype)[None], (bsz, N_META, D_MODEL))
        h = jnp.concatenate([meta, x], axis=1)
        for l in range(DEPTH):
            h = hybrid_layer(h, g_mix[l], w_in[l], ssd_conv_w[l], ssd_conv_b[l], dt_bias_f[l],
                             dt_bias_b[l], a_log_f[l], a_log_b[l], d_skip[l], ssd_norm_g[l],
                             q_norm_g[l], k_norm_g[l], rel_bias[l], meta_bias[l], w_br_ssd[l],
                             w_br_na[l], w_out[l], g_ffn[l], w_up[l], ffn_conv_w[l],
                             ffn_conv_b[l], w_down[l])
        return h[:, N_META:]

    y_prompt = run(x_prompt)
    y_sample = run(x_sample)
    return (y_prompt, y_sample)
```

```python
import functools

import numpy as np
import jax
import jax.numpy as jnp
from jax import lax
from jax.experimental import pallas as pl
from jax.experimental.pallas import tpu as pltpu

F32 = jnp.float32
BF16 = jnp.bfloat16

D_MODEL = 1024
N_META = 16
GRID_W = 64
D_INNER = 2048
SSD_HEADDIM = 64
SSD_HEADS = 32
SSD_GROUPS = 8
D_STATE = 128
D_CONV = 5
CHUNK = 128
D_XBC = D_INNER + 2 * SSD_GROUPS * D_STATE
NA_HEADS = 16
NA_HEADDIM = 64
D_NA = NA_HEADS * NA_HEADDIM
WIN_H = 8
WIN_W = 16
D_FF = 2816
FFN_CONV = 3
EPS = 1e-6

LANES = 128
META_PAD = CHUNK - N_META
HEAD_PAIRS_SSD = SSD_HEADS // 2
HEAD_PAIRS_NA = NA_HEADS // 2
NA_ROWS_PER_BLOCK = 4
NA_BLOCK = NA_ROWS_PER_BLOCK * GRID_W
NEG_BIG = -1e30

COL_XBC = 0
COL_Z = COL_XBC + D_XBC
COL_Q = COL_Z + D_INNER
COL_K = COL_Q + D_NA
COL_V = COL_K + D_NA
COL_GS = COL_V + D_NA
COL_GN = COL_GS + D_MODEL
D_PROJ = COL_GN + D_MODEL

VMEM_LIMIT = 56 * 1024 * 1024


def _cparams(sem):
    return pltpu.CompilerParams(dimension_semantics=sem, vmem_limit_bytes=VMEM_LIMIT)


def _sigmoid(x):
    return 1.0 / (1.0 + jnp.exp(-x))


def _softplus(x):
    return jnp.maximum(x, 0.0) + jnp.log(1.0 + jnp.exp(-jnp.abs(x)))


def _rmsnorm_rows(x, g):
    ms = jnp.mean(x * x, axis=-1, keepdims=True)
    return x * lax.rsqrt(ms + EPS) * g


def _inproj_kernel(x_ref, g_ref, w_ref, wdt_ref, out_ref, dt_ref, u_scr):
    @pl.when(pl.program_id(1) == 0)
    def _():
        u = _rmsnorm_rows(x_ref[...], g_ref[...]).astype(BF16)
        u_scr[...] = u
        dt_ref[...] = jnp.dot(u, wdt_ref[...], preferred_element_type=F32)

    out_ref[...] = jnp.dot(u_scr[...], w_ref[...], preferred_element_type=F32).astype(BF16)


def _inproj(x2d, g_mix, w_main, w_dt, tm):
    rows = x2d.shape[0]
    tn = 1024
    return pl.pallas_call(
        _inproj_kernel,
        grid=(rows // tm, D_PROJ // tn),
        in_specs=[
            pl.BlockSpec((tm, D_MODEL), lambda i, j: (i, 0)),
            pl.BlockSpec((1, D_MODEL), lambda i, j: (0, 0)),
            pl.BlockSpec((D_MODEL, tn), lambda i, j: (0, j)),
            pl.BlockSpec((D_MODEL, LANES), lambda i, j: (0, 0)),
        ],
        out_specs=[
            pl.BlockSpec((tm, tn), lambda i, j: (i, j)),
            pl.BlockSpec((tm, LANES), lambda i, j: (i, 0)),
        ],
        out_shape=[
            jax.ShapeDtypeStruct((rows, D_PROJ), BF16),
            jax.ShapeDtypeStruct((rows, LANES), F32),
        ],
        scratch_shapes=[pltpu.VMEM((tm, D_MODEL), BF16)],
        compiler_params=_cparams(("parallel", "arbitrary")),
        name="inproj",
    )(x2d, g_mix, w_main, w_dt)


HALO = 16


def _conv_kernel(tok_ref, prev_ref, next_ref, meta_ref, w_ref, b_ref, out_ref, ext_scr, *, nc):
    c = pl.program_id(1)
    is_meta = c == 0
    main = jnp.where(is_meta, meta_ref[...], tok_ref[...]).astype(F32)
    prev = jnp.where(c == 1, meta_ref[CHUNK - HALO:, :], prev_ref[...]).astype(F32)
    prev = jnp.where(is_meta, 0.0, prev)
    nxt = jnp.where(c == nc - 1, 0.0, next_ref[...].astype(F32))
    ext_scr[0:HALO, :] = prev
    ext_scr[HALO:HALO + CHUNK, :] = main
    ext_scr[HALO + CHUNK:, :] = nxt
    pad = (D_CONV - 1) // 2
    acc = b_ref[...] + w_ref[0:1, :] * ext_scr[pl.ds(HALO - pad, CHUNK), :]
    for k in range(1, D_CONV):
        acc = acc + w_ref[k:k + 1, :] * ext_scr[pl.ds(HALO - pad + k, CHUNK), :]
    out_ref[...] = (acc * _sigmoid(acc)).astype(BF16)


def _ssd_conv(proj_tok, proj_meta, conv_w, conv_b, bsz, T):
    nct = T // CHUNK
    nc = nct + 1
    hb = CHUNK // HALO
    return pl.pallas_call(
        functools.partial(_conv_kernel, nc=nc),
        grid=(bsz, nc),
        in_specs=[
            pl.BlockSpec((CHUNK, D_XBC), lambda b, c: (b * nct + jnp.maximum(c - 1, 0), 0)),
            pl.BlockSpec((HALO, D_XBC), lambda b, c: (jnp.maximum(b * nct * hb + (c - 1) * hb - 1, 0), 0)),
            pl.BlockSpec((HALO, D_XBC), lambda b, c: (b * nct * hb + jnp.minimum(c, nct - 1) * hb, 0)),
            pl.BlockSpec((CHUNK, D_XBC), lambda b, c: (0, 0)),
            pl.BlockSpec((8, D_XBC), lambda b, c: (0, 0)),
            pl.BlockSpec((1, D_XBC), lambda b, c: (0, 0)),
        ],
        out_specs=pl.BlockSpec((CHUNK, D_XBC), lambda b, c: (b * nc + c, 0)),
        out_shape=jax.ShapeDtypeStruct((bsz * nc * CHUNK, D_XBC), BF16),
        scratch_shapes=[pltpu.VMEM((CHUNK + 2 * HALO, D_XBC), F32)],
        compiler_params=_cparams(("parallel", "arbitrary")),
        name="ssd_conv",
    )(proj_tok, proj_tok, proj_tok, proj_meta, conv_w, conv_b)


def _ssd_prologue(c, dt_tok_ref, dt_meta_ref, dtb_ref, alog_ref):
    row = lax.broadcasted_iota(jnp.int32, (CHUNK, LANES), 0)
    col = lax.broadcasted_iota(jnp.int32, (CHUNK, LANES), 1)
    dt_raw = jnp.where(c == 0, dt_meta_ref[...], dt_tok_ref[...])
    dt = _softplus(dt_raw + dtb_ref[...])
    dt = jnp.where(jnp.logical_and(c == 0, row < META_PAD), 0.0, dt)
    a = dt * (-jnp.exp(alog_ref[...]))
    tri = (col <= row).astype(F32)
    cs = jnp.dot(tri, a, preferred_element_type=F32, precision=lax.Precision.HIGHEST)
    return row, col, dt, a, cs


def _lane_pair(col_lo, col_hi, lo_mask):
    return jnp.where(lo_mask, col_lo, col_hi)


def _ssd_fwd_kernel(xbc_ref, dt_tok_ref, dt_meta_ref, dtb_ref, alog_ref, y_ref, state_scr):
    c = pl.program_id(1)

    @pl.when(c == 0)
    def _():
        state_scr[...] = jnp.zeros_like(state_scr)

    row, col, dt, a, cs = _ssd_prologue(c, dt_tok_ref, dt_meta_ref, dtb_ref, alog_ref)
    ecs = cs - a
    cs_t, ecs_t, dt_t = cs.T, ecs.T, dt.T
    lower = row >= col
    strict_lower = row > col
    strict_upper = row < col
    lo_half = col < SSD_HEADDIM
    tot = cs[CHUNK - 1:CHUNK, :]

    for g in range(SSD_GROUPS):
        b_g = xbc_ref[:, D_INNER + g * D_STATE:D_INNER + (g + 1) * D_STATE]
        c_g = xbc_ref[:, D_INNER + SSD_GROUPS * D_STATE + g * D_STATE:
                      D_INNER + SSD_GROUPS * D_STATE + (g + 1) * D_STATE]
        cb = lax.dot_general(c_g, b_g, (((1,), (1,)), ((), ())), preferred_element_type=F32)
        st = state_scr[g]
        y_off = jnp.dot(c_g, st.astype(BF16), preferred_element_type=F32)
        xw_tiles, dec_tiles = [], []
        for pr in range(2):
            tile = g * 2 + pr
            x_pair = xbc_ref[:, tile * LANES:(tile + 1) * LANES]
            y_pair = jnp.zeros((CHUNK, LANES), F32)
            e_cols, w_cols, decs = [], [], []
            for hh in range(2):
                h = g * 4 + pr * 2 + hh
                hb = SSD_HEADS + h
                csf_col = cs[:, h:h + 1]
                e_mat = jnp.where(lower, csf_col - cs_t[h:h + 1, :], ecs_t[hb:hb + 1, :] - ecs[:, hb:hb + 1])
                dtf_row, dtb_row = dt_t[h:h + 1, :], dt_t[hb:hb + 1, :]
                dsel = jnp.where(strict_lower, dtf_row, jnp.where(strict_upper, dtb_row, dtf_row + dtb_row))
                m_mat = (cb * jnp.exp(e_mat) * dsel).astype(BF16)
                x_h = jnp.where(lo_half == (hh == 0), x_pair, jnp.zeros_like(x_pair))
                y_pair = y_pair + jnp.dot(m_mat, x_h, preferred_element_type=F32)
                e_cols.append(jnp.exp(csf_col))
                w_cols.append(dt[:, h:h + 1] * jnp.exp(tot[:, h:h + 1] - csf_col))
                decs.append(jnp.exp(tot[:, h:h + 1]))
            y_pair = y_pair + _lane_pair(e_cols[0], e_cols[1], lo_half) * y_off[:, pr * LANES:(pr + 1) * LANES]
            y_ref[:, tile * LANES:(tile + 1) * LANES] = y_pair
            xw_tiles.append((x_pair.astype(F32) * _lane_pair(w_cols[0], w_cols[1], lo_half)).astype(BF16))
            dec_tiles.append(_lane_pair(decs[0], decs[1], lo_half[0:1, :]))
        xw = jnp.concatenate(xw_tiles, axis=1)
        dec = jnp.concatenate(dec_tiles, axis=1)
        upd = lax.dot_general(b_g, xw, (((0,), (0,)), ((), ())), preferred_element_type=F32)
        state_scr[g] = st * dec + upd


def _ssd_bwd_kernel(xbc_ref, dt_tok_ref, dt_meta_ref, dtb_ref, alog_ref, z_tok_ref, z_meta_ref, yf_ref,
                    dskip_ref, ng_ref, y_tok_ref, y_meta_ref, state_scr, y_scr, *, nc):
    i = pl.program_id(1)
    c = nc - 1 - i

    @pl.when(i == 0)
    def _():
        state_scr[...] = jnp.zeros_like(state_scr)

    row, col, dt, a, cs = _ssd_prologue(c, dt_tok_ref, dt_meta_ref, dtb_ref, alog_ref)
    ecs = cs - a
    lo_half = col < SSD_HEADDIM
    tot = cs[CHUNK - 1:CHUNK, :]

    for g in range(SSD_GROUPS):
        b_g = xbc_ref[:, D_INNER + g * D_STATE:D_INNER + (g + 1) * D_STATE]
        c_g = xbc_ref[:, D_INNER + SSD_GROUPS * D_STATE + g * D_STATE:
                      D_INNER + SSD_GROUPS * D_STATE + (g + 1) * D_STATE]
        st = state_scr[g]
        y_off = jnp.dot(c_g, st.astype(BF16), preferred_element_type=F32)
        xw_tiles, dec_tiles = [], []
        for pr in range(2):
            tile = g * 2 + pr
            x_pair = xbc_ref[:, tile * LANES:(tile + 1) * LANES].astype(F32)
            e_cols, w_cols, decs = [], [], []
            for hh in range(2):
                hb = SSD_HEADS + g * 4 + pr * 2 + hh
                ecs_col = ecs[:, hb:hb + 1]
                e_cols.append(jnp.exp(tot[:, hb:hb + 1] - ecs_col))
                w_cols.append(dt[:, hb:hb + 1] * jnp.exp(ecs_col))
                decs.append(jnp.exp(tot[:, hb:hb + 1]))
            sl = slice(tile * LANES, (tile + 1) * LANES)
            y_pair = (yf_ref[:, sl] + _lane_pair(e_cols[0], e_cols[1], lo_half) * y_off[:, pr * LANES:(pr + 1) * LANES]
                      + dskip_ref[:, sl] * x_pair)
            z_pair = jnp.where(c == 0, z_meta_ref[:, sl], z_tok_ref[:, sl]).astype(F32)
            y_scr[:, sl] = y_pair * (z_pair * _sigmoid(z_pair))
            xw_tiles.append((x_pair * _lane_pair(w_cols[0], w_cols[1], lo_half)).astype(BF16))
            dec_tiles.append(_lane_pair(decs[0], decs[1], lo_half[0:1, :]))
        xw = jnp.concatenate(xw_tiles, axis=1)
        dec = jnp.concatenate(dec_tiles, axis=1)
        upd = lax.dot_general(b_g, xw, (((0,), (0,)), ((), ())), preferred_element_type=F32)
        state_scr[g] = st * dec + upd

    gw = D_INNER // SSD_GROUPS
    for g in range(SSD_GROUPS):
        yg = y_scr[:, g * gw:(g + 1) * gw]
        yg = yg * lax.rsqrt(jnp.mean(yg * yg, axis=-1, keepdims=True) + EPS) * ng_ref[:, g * gw:(g + 1) * gw]
        y_scr[:, g * gw:(g + 1) * gw] = yg

    @pl.when(c > 0)
    def _():
        y_tok_ref[...] = y_scr[...].astype(BF16)

    @pl.when(c == 0)
    def _():
        y_meta_ref[...] = y_scr[...].astype(BF16)


def _ssd(xbc, dt_tok, dt_meta, proj_tok, proj_meta, dtb, alog, dskip, ng, bsz, T):
    nct = T // CHUNK
    nc = nct + 1
    small = lambda b, c: (0, 0)
    state = pltpu.VMEM((SSD_GROUPS, D_STATE, 4 * SSD_HEADDIM), F32)
    y_f = pl.pallas_call(
        _ssd_fwd_kernel,
        grid=(bsz, nc),
        in_specs=[
            pl.BlockSpec((CHUNK, D_XBC), lambda b, c: (b * nc + c, 0)),
            pl.BlockSpec((CHUNK, LANES), lambda b, c: (b * nct + jnp.maximum(c - 1, 0), 0)),
            pl.BlockSpec((CHUNK, LANES), small),
            pl.BlockSpec((1, LANES), small),
            pl.BlockSpec((1, LANES), small),
        ],
        out_specs=pl.BlockSpec((CHUNK, D_INNER), lambda b, c: (b * nc + c, 0)),
        out_shape=jax.ShapeDtypeStruct((bsz * nc * CHUNK, D_INNER), F32),
        scratch_shapes=[state],
        compiler_params=_cparams(("parallel", "arbitrary")),
        name="ssd_fwd",
    )(xbc, dt_tok, dt_meta, dtb, alog)

    zb = COL_Z // D_INNER
    rev = lambda b, i: nc - 1 - i
    tok_blk = lambda b, i: b * nct + jnp.maximum(rev(b, i) - 1, 0)
    y_tok, y_meta = pl.pallas_call(
        functools.partial(_ssd_bwd_kernel, nc=nc),
        grid=(bsz, nc),
        in_specs=[
            pl.BlockSpec((CHUNK, D_XBC), lambda b, i: (b * nc + rev(b, i), 0)),
            pl.BlockSpec((CHUNK, LANES), lambda b, i: (tok_blk(b, i), 0)),
            pl.BlockSpec((CHUNK, LANES), small),
            pl.BlockSpec((1, LANES), small),
            pl.BlockSpec((1, LANES), small),
            pl.BlockSpec((CHUNK, D_INNER), lambda b, i: (tok_blk(b, i), zb)),
            pl.BlockSpec((CHUNK, D_INNER), lambda b, i: (0, zb)),
            pl.BlockSpec((CHUNK, D_INNER), lambda b, i: (b * nc + rev(b, i), 0)),
            pl.BlockSpec((1, D_INNER), small),
            pl.BlockSpec((1, D_INNER), small),
        ],
        out_specs=[
            pl.BlockSpec((CHUNK, D_INNER), lambda b, i: (tok_blk(b, i), 0)),
            pl.BlockSpec((CHUNK, D_INNER), lambda b, i: (b, 0)),
        ],
        out_shape=[
            jax.ShapeDtypeStruct((bsz * T, D_INNER), BF16),
            jax.ShapeDtypeStruct((bsz * CHUNK, D_INNER), BF16),
        ],
        scratch_shapes=[state, pltpu.VMEM((CHUNK, D_INNER), F32)],
        compiler_params=_cparams(("parallel", "arbitrary")),
        name="ssd_bwd",
    )(xbc, dt_tok, dt_meta, dtb, alog, proj_tok, proj_meta, y_f, dskip, ng)
    return y_tok, y_meta


def _head_rmsnorm(x, gain, lo_half):
    sq = x * x
    s_lo = jnp.sum(jnp.where(lo_half, sq, 0.0), axis=-1, keepdims=True)
    s_all = jnp.sum(sq, axis=-1, keepdims=True)
    r_lo = lax.rsqrt(s_lo * (1.0 / NA_HEADDIM) + EPS)
    r_hi = lax.rsqrt((s_all - s_lo) * (1.0 / NA_HEADDIM) + EPS)
    return x * jnp.where(lo_half, r_lo, r_hi) * gain


def _softmax_pv(s_win, s_meta, v_win, v_meta):
    m = jnp.maximum(jnp.max(s_win, axis=-1, keepdims=True), jnp.max(s_meta, axis=-1, keepdims=True))
    p_win = jnp.exp(s_win - m)
    p_meta = jnp.exp(s_meta - m)
    den = jnp.sum(p_win, axis=-1, keepdims=True) + jnp.sum(p_meta, axis=-1, keepdims=True)
    o = (jnp.dot(p_win.astype(BF16), v_win, preferred_element_type=F32)
         + jnp.dot(p_meta.astype(BF16), v_meta, preferred_element_type=F32))
    return o / den


def _na_kernel(q_ref, k0_ref, k1_ref, k2_ref, v0_ref, v1_ref, v2_ref, km_ref, vm_ref, tb_ref, mb_ref,
               qg_ref, kg_ref, out_ref, k_scr, v_scr, *, n_blocks):
    j = pl.program_id(2)
    lo_half = lax.broadcasted_iota(jnp.int32, (1, LANES), 1) < NA_HEADDIM
    kg = kg_ref[...]
    for t, (k_ref, v_ref) in enumerate(((k0_ref, v0_ref), (k1_ref, v1_ref), (k2_ref, v2_ref))):
        k_scr[t * NA_BLOCK:(t + 1) * NA_BLOCK, :] = _head_rmsnorm(k_ref[...].astype(F32), kg, lo_half).astype(BF16)
        v_scr[t * NA_BLOCK:(t + 1) * NA_BLOCK, :] = v_ref[...]
    k_meta = _head_rmsnorm(km_ref[META_PAD:, :].astype(F32), kg, lo_half).astype(BF16)
    v_meta = vm_ref[META_PAD:, :]
    q = _head_rmsnorm(q_ref[...].astype(F32), qg_ref[...], lo_half) * (NA_HEADDIM ** -0.5)
    q = q.astype(BF16)
    nt = (((1,), (1,)), ((), ()))
    for r in range(NA_ROWS_PER_BLOCK):
        rho = jnp.where(j == 0, NA_ROWS_PER_BLOCK, jnp.where(j == n_blocks - 1, 0, r))
        d0 = rho - r + (WIN_H - 1 - WIN_H // 2)
        off = pl.multiple_of(rho * GRID_W, GRID_W)
        k_win = k_scr[pl.ds(off, WIN_H * GRID_W), :]
        v_win = v_scr[pl.ds(off, WIN_H * GRID_W), :]
        q_r = q[r * GRID_W:(r + 1) * GRID_W, :]
        outs = []
        for hh in range(2):
            q_h = jnp.where(lo_half == (hh == 0), q_r, jnp.zeros_like(q_r))
            s_win = lax.dot_general(q_h, k_win, nt, preferred_element_type=F32) + tb_ref[0, hh, d0]
            s_meta = lax.dot_general(q_h, k_meta, nt, preferred_element_type=F32) + mb_ref[0, hh:hh + 1, 0:N_META]
            outs.append(_softmax_pv(s_win, s_meta, v_win, v_meta))
        out_ref[r * GRID_W:(r + 1) * GRID_W, :] = jnp.where(lo_half, outs[0], outs[1]).astype(BF16)


def _na_meta_kernel(q_ref, k_ref, v_ref, mb_ref, qg_ref, kg_ref, out_ref):
    lo_half = lax.broadcasted_iota(jnp.int32, (1, LANES), 1) < NA_HEADDIM
    k_meta = _head_rmsnorm(k_ref[META_PAD:, :].astype(F32), kg_ref[...], lo_half).astype(BF16)
    v_meta = v_ref[META_PAD:, :]
    q = (_head_rmsnorm(q_ref[...].astype(F32), qg_ref[...], lo_half) * (NA_HEADDIM ** -0.5)).astype(BF16)
    nt = (((1,), (1,)), ((), ()))
    outs = []
    for hh in range(2):
        q_h = jnp.where(lo_half == (hh == 0), q, jnp.zeros_like(q))
        s = lax.dot_general(q_h, k_meta, nt, preferred_element_type=F32) + mb_ref[0, hh:hh + 1, 0:N_META]
        p = jnp.exp(s - jnp.max(s, axis=-1, keepdims=True))
        o = jnp.dot(p.astype(BF16), v_meta, preferred_element_type=F32) / jnp.sum(p, axis=-1, keepdims=True)
        outs.append(o)
    out_ref[...] = jnp.where(lo_half, outs[0], outs[1]).astype(BF16)


def _na(proj_tok, proj_meta, tb, mb, qg, kg, bsz, T):
    nb = T // NA_BLOCK
    assert nb >= 2
    qc, kc, vc = COL_Q // LANES, COL_K // LANES, COL_V // LANES
    blk = (NA_BLOCK, LANES)
    small = lambda p, b, j: (0, 0)
    return pl.pallas_call(
        functools.partial(_na_kernel, n_blocks=nb),
        grid=(HEAD_PAIRS_NA, bsz, nb),
        in_specs=[
            pl.BlockSpec(blk, lambda p, b, j: (b * nb + j, qc + p)),
            pl.BlockSpec(blk, lambda p, b, j: (b * nb + jnp.maximum(j - 1, 0), kc + p)),
            pl.BlockSpec(blk, lambda p, b, j: (b * nb + j, kc + p)),
            pl.BlockSpec(blk, lambda p, b, j: (b * nb + jnp.minimum(j + 1, nb - 1), kc + p)),
            pl.BlockSpec(blk, lambda p, b, j: (b * nb + jnp.maximum(j - 1, 0), vc + p)),
            pl.BlockSpec(blk, lambda p, b, j: (b * nb + j, vc + p)),
            pl.BlockSpec(blk, lambda p, b, j: (b * nb + jnp.minimum(j + 1, nb - 1), vc + p)),
            pl.BlockSpec((CHUNK, LANES), lambda p, b, j: (0, kc + p)),
            pl.BlockSpec((CHUNK, LANES), lambda p, b, j: (0, vc + p)),
            pl.BlockSpec((1, 2, WIN_H, GRID_W, WIN_H * GRID_W), lambda p, b, j: (p, 0, 0, 0, 0)),
            pl.BlockSpec((1, 8, LANES), lambda p, b, j: (p, 0, 0)),
            pl.BlockSpec((1, LANES), small),
            pl.BlockSpec((1, LANES), small),
        ],
        out_specs=pl.BlockSpec(blk, lambda p, b, j: (b * nb + j, p)),
        out_shape=jax.ShapeDtypeStruct((bsz * T, D_NA), BF16),
        scratch_shapes=[pltpu.VMEM((3 * NA_BLOCK, LANES), BF16), pltpu.VMEM((3 * NA_BLOCK, LANES), BF16)],
        compiler_params=_cparams(("arbitrary", "arbitrary", "arbitrary")),
        name="natten",
    )(proj_tok, proj_tok, proj_tok, proj_tok, proj_tok, proj_tok, proj_tok, proj_meta, proj_meta, tb, mb, qg, kg)


def _na_meta(proj_meta, mb, qg, kg):
    qc, kc, vc = COL_Q // LANES, COL_K // LANES, COL_V // LANES
    blk = (CHUNK, LANES)
    return pl.pallas_call(
        _na_meta_kernel,
        grid=(HEAD_PAIRS_NA,),
        in_specs=[
            pl.BlockSpec(blk, lambda p: (0, qc + p)),
            pl.BlockSpec(blk, lambda p: (0, kc + p)),
            pl.BlockSpec(blk, lambda p: (0, vc + p)),
            pl.BlockSpec((1, 8, LANES), lambda p: (p, 0, 0)),
            pl.BlockSpec((1, LANES), lambda p: (0, 0)),
            pl.BlockSpec((1, LANES), lambda p: (0, 0)),
        ],
        out_specs=pl.BlockSpec(blk, lambda p: (0, p)),
        out_shape=jax.ShapeDtypeStruct((CHUNK, D_NA), BF16),
        compiler_params=_cparams(("arbitrary",)),
        name="natten_meta",
    )(proj_meta, proj_meta, proj_meta, mb, qg, kg)


def _rel_bias_table(rel_bias):
    w = np.arange(GRID_W)
    cstart = np.clip(w - WIN_W // 2, 0, GRID_W - WIN_W)
    cc = np.arange(GRID_W)
    inside = (cc[None, :] >= cstart[:, None]) & (cc[None, :] < cstart[:, None] + WIN_W)
    cidx = np.clip(cc[None, :] - w[:, None] + (WIN_W - 1), 0, 2 * WIN_W - 2)
    ridx = np.arange(WIN_H)[:, None] + np.arange(WIN_H)[None, :]
    g = rel_bias[:, ridx[:, :, None, None], cidx[None, None, :, :]]
    g = jnp.where(inside[None, None, None], g, NEG_BIG)
    g = g.transpose(0, 1, 3, 2, 4).reshape(NA_HEADS, WIN_H, GRID_W, WIN_H * GRID_W)
    return g.reshape(HEAD_PAIRS_NA, 2, WIN_H, GRID_W, WIN_H * GRID_W)


def _merge_kernel(ys_ref, yn_ref, gs_ref, gn_ref, h_ref, wbs_ref, wbn_ref, wo_ref, out_ref):
    m_s = jnp.dot(ys_ref[...], wbs_ref[...], preferred_element_type=F32)
    m_n = jnp.dot(yn_ref[...], wbn_ref[...], preferred_element_type=F32)
    merged = _sigmoid(gs_ref[...].astype(F32)) * m_s + _sigmoid(gn_ref[...].astype(F32)) * m_n
    out_ref[...] = h_ref[...] + jnp.dot(merged.astype(BF16), wo_ref[...], preferred_element_type=F32)


def _merge(y_ssd, y_na, proj, h, wbs, wbn, wo, tm, shared_rows):
    rows = y_ssd.shape[0]
    gsc, gnc = COL_GS // D_MODEL, COL_GN // D_MODEL
    r = (lambda i: 0) if shared_rows else (lambda i: i)
    full = lambda i: (0, 0)
    return pl.pallas_call(
        _merge_kernel,
        grid=(rows // tm,),
        in_specs=[
            pl.BlockSpec((tm, D_INNER), lambda i: (i, 0)),
            pl.BlockSpec((tm, D_NA), lambda i: (r(i), 0)),
            pl.BlockSpec((tm, D_MODEL), lambda i: (r(i), gsc)),
            pl.BlockSpec((tm, D_MODEL), lambda i: (r(i), gnc)),
            pl.BlockSpec((tm, D_MODEL), lambda i: (r(i), 0)),
            pl.BlockSpec((D_INNER, D_MODEL), full),
            pl.BlockSpec((D_NA, D_MODEL), full),
            pl.BlockSpec((D_MODEL, D_MODEL), full),
        ],
        out_specs=pl.BlockSpec((tm, D_MODEL), lambda i: (i, 0)),
        out_shape=jax.ShapeDtypeStruct((rows, D_MODEL), F32),
        compiler_params=_cparams(("parallel",)),
        name="merge",
    )(y_ssd, y_na, proj, proj, h, wbs, wbn, wo)


FFN_HALO = 16
FFN_TF = 256


def _ffn_kernel(h_ref, prev_ref, next_ref, meta_ref, g_ref, wa_ref, wg_ref, cwa_ref, cwg_ref, cba_ref, cbg_ref,
                wd_ref, out_ref, u_scr, a_scr, g_scr, acc_scr, *, tiles_per_seq, tm):
    i = pl.program_id(0)
    k = pl.program_id(1)
    pos = i % tiles_per_seq

    @pl.when(k == 0)
    def _():
        gain = g_ref[...]
        prev = jnp.where(pos == 0, meta_ref[...], prev_ref[...])
        nxt = jnp.where(pos == tiles_per_seq - 1, 0.0, next_ref[...])
        u_scr[0:FFN_HALO, :] = _rmsnorm_rows(prev, gain).astype(BF16)
        u_scr[FFN_HALO:FFN_HALO + tm, :] = _rmsnorm_rows(h_ref[...], gain).astype(BF16)
        u_scr[FFN_HALO + tm:, :] = _rmsnorm_rows(nxt, gain).astype(BF16)
        acc_scr[...] = jnp.zeros_like(acc_scr)

    u = u_scr[...]
    a_scr[...] = jnp.dot(u, wa_ref[...], preferred_element_type=F32)
    g_scr[...] = jnp.dot(u, wg_ref[...], preferred_element_type=F32)

    def conv(scr, cw_ref, cb_ref):
        acc = cb_ref[...]
        for t in range(FFN_CONV):
            acc = acc + cw_ref[t:t + 1, :] * scr[pl.ds(FFN_HALO - 1 + t, tm), :]
        return acc

    a = conv(a_scr, cwa_ref, cba_ref)
    g = conv(g_scr, cwg_ref, cbg_ref)
    act = (g * _sigmoid(g) * a).astype(BF16)
    acc_scr[...] += jnp.dot(act, wd_ref[...], preferred_element_type=F32)

    @pl.when(k == pl.num_programs(1) - 1)
    def _():
        out_ref[...] = h_ref[...] + acc_scr[...]


def _ffn(h1, h1_meta, g_ffn, w_up, cw, cb, w_down, bsz, T, tm):
    rows = bsz * T
    tps = T // tm
    nk = D_FF // FFN_TF
    hb = tm // FFN_HALO
    mb = CHUNK // FFN_HALO
    nblk8 = rows // FFN_HALO
    return pl.pallas_call(
        functools.partial(_ffn_kernel, tiles_per_seq=tps, tm=tm),
        grid=(rows // tm, nk),
        in_specs=[
            pl.BlockSpec((tm, D_MODEL), lambda i, k: (i, 0)),
            pl.BlockSpec((FFN_HALO, D_MODEL), lambda i, k: (jnp.maximum(i * hb - 1, 0), 0)),
            pl.BlockSpec((FFN_HALO, D_MODEL), lambda i, k: (jnp.minimum((i + 1) * hb, nblk8 - 1), 0)),
            pl.BlockSpec((FFN_HALO, D_MODEL), lambda i, k: ((i // tps) * mb + mb - 1, 0)),
            pl.BlockSpec((1, D_MODEL), lambda i, k: (0, 0)),
            pl.BlockSpec((D_MODEL, FFN_TF), lambda i, k: (0, k)),
            pl.BlockSpec((D_MODEL, FFN_TF), lambda i, k: (0, nk + k)),
            pl.BlockSpec((8, FFN_TF), lambda i, k: (0, k)),
            pl.BlockSpec((8, FFN_TF), lambda i, k: (0, nk + k)),
            pl.BlockSpec((1, FFN_TF), lambda i, k: (0, k)),
            pl.BlockSpec((1, FFN_TF), lambda i, k: (0, nk + k)),
            pl.BlockSpec((FFN_TF, D_MODEL), lambda i, k: (k, 0)),
        ],
        out_specs=pl.BlockSpec((tm, D_MODEL), lambda i, k: (i, 0)),
        out_shape=jax.ShapeDtypeStruct((rows, D_MODEL), F32),
        scratch_shapes=[
            pltpu.VMEM((tm + 2 * FFN_HALO, D_MODEL), BF16),
            pltpu.VMEM((tm + 2 * FFN_HALO, FFN_TF), F32),
            pltpu.VMEM((tm + 2 * FFN_HALO, FFN_TF), F32),
            pltpu.VMEM((tm, D_MODEL), F32),
        ],
        compiler_params=_cparams(("parallel", "arbitrary")),
        name="ffn",
    )(h1, h1, h1, h1_meta, g_ffn, w_up, w_up, cw, cw, cb, cb, w_down)


def _pad_rows(a, n):
    return jnp.pad(a, ((0, n - a.shape[0]), (0, 0)))


def _row_tile(rows, cap):
    tm = min(cap, rows)
    assert rows % tm == 0
    return tm


def kernel(x_prompt, x_sample, meta_tokens, g_mix, w_in, ssd_conv_w, ssd_conv_b, dt_bias_f, dt_bias_b,
           a_log_f, a_log_b, d_skip, ssd_norm_g, q_norm_g, k_norm_g, rel_bias, meta_bias, w_br_ssd,
           w_br_na, w_out, g_ffn, w_up, ffn_conv_w, ffn_conv_b, w_down):
    assert g_mix.shape[0] == 1, "single-layer block"
    w = w_in[0]
    o_z, o_xbc = 0, D_INNER
    o_dtf = o_xbc + D_XBC
    o_q = o_dtf + 2 * SSD_HEADS
    w_main = jnp.concatenate([w[:, o_xbc:o_dtf], w[:, o_z:o_xbc], w[:, o_q:]], axis=1).astype(BF16)
    w_dt = jnp.pad(w[:, o_dtf:o_q], ((0, 0), (0, LANES - 2 * SSD_HEADS))).astype(BF16)
    g_mix2 = g_mix.astype(F32)
    lane_pad = lambda v: jnp.pad(v, (0, LANES - v.shape[0]))[None, :].astype(F32)
    dtb = lane_pad(jnp.concatenate([dt_bias_f[0], dt_bias_b[0]]))
    alog = lane_pad(jnp.concatenate([a_log_f[0], a_log_b[0]]))
    dskip = jnp.repeat(d_skip[0].astype(F32), SSD_HEADDIM)[None, :]
    ng = ssd_norm_g.astype(F32)
    conv_w = _pad_rows(ssd_conv_w[0].astype(F32), 8)
    conv_b = ssd_conv_b.astype(F32)
    qg = jnp.tile(q_norm_g[0].astype(F32), 2)[None, :]
    kg = jnp.tile(k_norm_g[0].astype(F32), 2)[None, :]
    tb = _rel_bias_table(rel_bias[0].astype(F32))
    mb = jnp.pad(meta_bias[0].astype(F32).reshape(HEAD_PAIRS_NA, 2, N_META), ((0, 0), (0, 6), (0, LANES - N_META)))
    wbs, wbn, wo = w_br_ssd[0].astype(BF16), w_br_na[0].astype(BF16), w_out[0].astype(BF16)
    w_up_b, w_down_b = w_up[0].astype(BF16), w_down[0].astype(BF16)
    ffn_cw = _pad_rows(ffn_conv_w[0].astype(F32), 8)
    ffn_cb = ffn_conv_b.astype(F32)

    h_meta = jnp.concatenate([jnp.zeros((META_PAD, D_MODEL), F32), meta_tokens.astype(F32)], axis=0)
    proj_meta, dt_meta = _inproj(h_meta, g_mix2, w_main, w_dt, CHUNK)
    y_na_meta = _na_meta(proj_meta, mb, qg, kg)

    def run(x):
        bsz, T, _ = x.shape
        assert T % NA_BLOCK == 0 and T % CHUNK == 0
        x2d = x.reshape(bsz * T, D_MODEL)
        tm = _row_tile(bsz * T, 1024)
        proj_tok, dt_tok = _inproj(x2d, g_mix2, w_main, w_dt, tm)
        xbc = _ssd_conv(proj_tok, proj_meta, conv_w, conv_b, bsz, T)
        y_ssd, y_ssd_meta = _ssd(xbc, dt_tok, dt_meta, proj_tok, proj_meta, dtb, alog, dskip, ng, bsz, T)
        y_na = _na(proj_tok, proj_meta, tb, mb, qg, kg, bsz, T)
        h1 = _merge(y_ssd, y_na, proj_tok, x2d, wbs, wbn, wo, tm, False)
        h1_meta = _merge(y_ssd_meta, y_na_meta, proj_meta, h_meta, wbs, wbn, wo, CHUNK, True)
        tf = _row_tile(T, 1024)
        out = _ffn(h1, h1_meta, g_ffn.astype(F32), w_up_b, ffn_cw, ffn_cb, w_down_b, bsz, T, tf)
        return out.reshape(bsz, T, D_MODEL)

    return (run(x_prompt), run(x_sample))
```

```python
import functools

import numpy as np
import jax
import jax.numpy as jnp
from jax import lax
from jax.experimental import pallas as pl
from jax.experimental.pallas import tpu as pltpu

F32 = jnp.float32
BF16 = jnp.bfloat16

D_MODEL = 1024
N_META = 16
GRID_W = 64
D_INNER = 2048
SSD_HEADDIM = 64
SSD_HEADS = 32
SSD_GROUPS = 8
D_STATE = 128
D_CONV = 5
CHUNK = 128
D_XBC = D_INNER + 2 * SSD_GROUPS * D_STATE
NA_HEADS = 16
NA_HEADDIM = 64
D_NA = NA_HEADS * NA_HEADDIM
WIN_H = 8
WIN_W = 16
D_FF = 2816
FFN_CONV = 3
EPS = 1e-6

LANES = 128
META_PAD = CHUNK - N_META
HEAD_PAIRS_SSD = SSD_HEADS // 2
HEAD_PAIRS_NA = NA_HEADS // 2
NA_ROWS_PER_BLOCK = 16
NA_BLOCK = NA_ROWS_PER_BLOCK * GRID_W
NEG_BIG = -1e30

COL_XBC = 0
COL_Z = COL_XBC + D_XBC
COL_Q = COL_Z + D_INNER
COL_K = COL_Q + D_NA
COL_V = COL_K + D_NA
COL_GS = COL_V + D_NA
COL_GN = COL_GS + D_MODEL
D_PROJ = COL_GN + D_MODEL

VMEM_LIMIT = 56 * 1024 * 1024


def _cparams(sem):
    return pltpu.CompilerParams(dimension_semantics=sem, vmem_limit_bytes=VMEM_LIMIT)


def _sigmoid(x):
    return 1.0 / (1.0 + jnp.exp(-x))


def _softplus(x):
    return jnp.maximum(x, 0.0) + jnp.log(1.0 + jnp.exp(-jnp.abs(x)))


def _rmsnorm_rows(x, g):
    ms = jnp.mean(x * x, axis=-1, keepdims=True)
    return x * lax.rsqrt(ms + EPS) * g


def _inproj_kernel(x_ref, g_ref, w_ref, wdt_ref, out_ref, dt_ref, u_scr):
    @pl.when(pl.program_id(1) == 0)
    def _():
        u = _rmsnorm_rows(x_ref[...], g_ref[...]).astype(BF16)
        u_scr[...] = u
        dt_ref[...] = jnp.dot(u, wdt_ref[...], preferred_element_type=F32)

    out_ref[...] = jnp.dot(u_scr[...], w_ref[...], preferred_element_type=F32).astype(BF16)


def _inproj(x2d, g_mix, w_main, w_dt, tm):
    rows = x2d.shape[0]
    tn = 1024
    return pl.pallas_call(
        _inproj_kernel,
        grid=(rows // tm, D_PROJ // tn),
        in_specs=[
            pl.BlockSpec((tm, D_MODEL), lambda i, j: (i, 0)),
            pl.BlockSpec((1, D_MODEL), lambda i, j: (0, 0)),
            pl.BlockSpec((D_MODEL, tn), lambda i, j: (0, j)),
            pl.BlockSpec((D_MODEL, LANES), lambda i, j: (0, 0)),
        ],
        out_specs=[
            pl.BlockSpec((tm, tn), lambda i, j: (i, j)),
            pl.BlockSpec((tm, LANES), lambda i, j: (i, 0)),
        ],
        out_shape=[
            jax.ShapeDtypeStruct((rows, D_PROJ), BF16),
            jax.ShapeDtypeStruct((rows, LANES), F32),
        ],
        scratch_shapes=[pltpu.VMEM((tm, D_MODEL), BF16)],
        compiler_params=_cparams(("parallel", "arbitrary")),
        name="inproj",
    )(x2d, g_mix, w_main, w_dt)


HALO = 16


def _conv_kernel(tok_ref, prev_ref, next_ref, meta_ref, w_ref, b_ref, out_ref, *, nc):
    c = pl.program_id(1)
    is_meta = c == 0
    pad = (D_CONV - 1) // 2
    n_ext = CHUNK + 2 * HALO
    strip = 4 * LANES
    for s in range(D_XBC // strip):
        cols = slice(s * strip, (s + 1) * strip)
        main = jnp.where(is_meta, meta_ref[:, cols], tok_ref[:, cols])
        prev = jnp.where(c == 1, meta_ref[CHUNK - HALO:, cols], prev_ref[:, cols])
        prev = jnp.where(is_meta, jnp.zeros_like(prev), prev)
        nxt = jnp.where(c == nc - 1, jnp.zeros_like(prev), next_ref[:, cols])
        ext = jnp.concatenate([prev, main, nxt], axis=0).astype(F32)
        acc = w_ref[pad:pad + 1, cols] * ext + b_ref[:, cols]
        for k in range(D_CONV):
            if k != pad:
                acc = acc + pltpu.roll(w_ref[k:k + 1, cols] * ext, (pad - k) % n_ext, 0)
        acc = acc[HALO:HALO + CHUNK]
        out_ref[:, cols] = (acc * _sigmoid(acc)).astype(BF16)


def _ssd_conv(proj_tok, proj_meta, conv_w, conv_b, bsz, T):
    nct = T // CHUNK
    nc = nct + 1
    hb = CHUNK // HALO
    return pl.pallas_call(
        functools.partial(_conv_kernel, nc=nc),
        grid=(bsz, nc),
        in_specs=[
            pl.BlockSpec((CHUNK, D_XBC), lambda b, c: (b * nct + jnp.maximum(c - 1, 0), 0)),
            pl.BlockSpec((HALO, D_XBC), lambda b, c: (jnp.maximum(b * nct * hb + (c - 1) * hb - 1, 0), 0)),
            pl.BlockSpec((HALO, D_XBC), lambda b, c: (b * nct * hb + jnp.minimum(c, nct - 1) * hb, 0)),
            pl.BlockSpec((CHUNK, D_XBC), lambda b, c: (0, 0)),
            pl.BlockSpec((8, D_XBC), lambda b, c: (0, 0)),
            pl.BlockSpec((1, D_XBC), lambda b, c: (0, 0)),
        ],
        out_specs=pl.BlockSpec((CHUNK, D_XBC), lambda b, c: (b * nc + c, 0)),
        out_shape=jax.ShapeDtypeStruct((bsz * nc * CHUNK, D_XBC), BF16),
        compiler_params=_cparams(("parallel", "arbitrary")),
        name="ssd_conv",
    )(proj_tok, proj_tok, proj_tok, proj_meta, conv_w, conv_b)


def _ssd_prologue(c, dt_tok_ref, dt_meta_ref, dtb_ref, alog_ref):
    row = lax.broadcasted_iota(jnp.int32, (CHUNK, LANES), 0)
    col = lax.broadcasted_iota(jnp.int32, (CHUNK, LANES), 1)
    dt_raw = jnp.where(c == 0, dt_meta_ref[...], dt_tok_ref[...])
    dt = _softplus(dt_raw + dtb_ref[...])
    dt = jnp.where(jnp.logical_and(c == 0, row < META_PAD), 0.0, dt)
    a = dt * (-jnp.exp(alog_ref[...]))
    tri = (col <= row).astype(F32)
    cs = jnp.dot(tri, a, preferred_element_type=F32, precision=lax.Precision.HIGHEST)
    return row, col, dt, a, cs


def _lane_pair(col_lo, col_hi, lo_mask):
    return jnp.where(lo_mask, col_lo, col_hi)


def _ssd_fwd_kernel(xbc_ref, dt_tok_ref, dt_meta_ref, dtb_ref, alog_ref, y_ref, state_scr):
    c = pl.program_id(1)

    @pl.when(c == 0)
    def _():
        state_scr[...] = jnp.zeros_like(state_scr)

    row, col, dt, a, cs = _ssd_prologue(c, dt_tok_ref, dt_meta_ref, dtb_ref, alog_ref)
    ecs = cs - a
    cs_t, ecs_t, dt_t = cs.T, ecs.T, dt.T
    lower = row >= col
    strict_lower = row > col
    strict_upper = row < col
    lo_half = col < SSD_HEADDIM
    tot = cs[CHUNK - 1:CHUNK, :]

    for g in range(SSD_GROUPS):
        b_g = xbc_ref[:, D_INNER + g * D_STATE:D_INNER + (g + 1) * D_STATE]
        c_g = xbc_ref[:, D_INNER + SSD_GROUPS * D_STATE + g * D_STATE:
                      D_INNER + SSD_GROUPS * D_STATE + (g + 1) * D_STATE]
        cb = lax.dot_general(c_g, b_g, (((1,), (1,)), ((), ())), preferred_element_type=F32)
        st = state_scr[g]
        y_off = jnp.dot(c_g, st.astype(BF16), preferred_element_type=F32)
        xw_tiles, dec_tiles = [], []
        for pr in range(2):
            tile = g * 2 + pr
            x_pair = xbc_ref[:, tile * LANES:(tile + 1) * LANES]
            y_pair = jnp.zeros((CHUNK, LANES), F32)
            e_cols, w_cols, decs = [], [], []
            for hh in range(2):
                h = g * 4 + pr * 2 + hh
                hb = SSD_HEADS + h
                csf_col = cs[:, h:h + 1]
                e_mat = jnp.where(lower, csf_col - cs_t[h:h + 1, :], ecs_t[hb:hb + 1, :] - ecs[:, hb:hb + 1])
                dtf_row, dtb_row = dt_t[h:h + 1, :], dt_t[hb:hb + 1, :]
                dsel = jnp.where(strict_lower, dtf_row, jnp.where(strict_upper, dtb_row, dtf_row + dtb_row))
                m_mat = (cb * jnp.exp(e_mat) * dsel).astype(BF16)
                x_h = jnp.where(lo_half == (hh == 0), x_pair, jnp.zeros_like(x_pair))
                y_pair = y_pair + jnp.dot(m_mat, x_h, preferred_element_type=F32)
                e_cols.append(jnp.exp(csf_col))
                w_cols.append(dt[:, h:h + 1] * jnp.exp(tot[:, h:h + 1] - csf_col))
                decs.append(jnp.exp(tot[:, h:h + 1]))
            y_pair = y_pair + _lane_pair(e_cols[0], e_cols[1], lo_half) * y_off[:, pr * LANES:(pr + 1) * LANES]
            y_ref[:, tile * LANES:(tile + 1) * LANES] = y_pair
            xw_tiles.append((x_pair.astype(F32) * _lane_pair(w_cols[0], w_cols[1], lo_half)).astype(BF16))
            dec_tiles.append(_lane_pair(decs[0], decs[1], lo_half[0:1, :]))
        xw = jnp.concatenate(xw_tiles, axis=1)
        dec = jnp.concatenate(dec_tiles, axis=1)
        upd = lax.dot_general(b_g, xw, (((0,), (0,)), ((), ())), preferred_element_type=F32)
        state_scr[g] = st * dec + upd


def _ssd_bwd_kernel(xbc_ref, dt_tok_ref, dt_meta_ref, dtb_ref, alog_ref, z_tok_ref, z_meta_ref, yf_ref,
                    dskip_ref, ng_ref, y_tok_ref, y_meta_ref, state_scr, y_scr, *, nc):
    i = pl.program_id(1)
    c = nc - 1 - i

    @pl.when(i == 0)
    def _():
        state_scr[...] = jnp.zeros_like(state_scr)

    row, col, dt, a, cs = _ssd_prologue(c, dt_tok_ref, dt_meta_ref, dtb_ref, alog_ref)
    ecs = cs - a
    lo_half = col < SSD_HEADDIM
    tot = cs[CHUNK - 1:CHUNK, :]

    for g in range(SSD_GROUPS):
        b_g = xbc_ref[:, D_INNER + g * D_STATE:D_INNER + (g + 1) * D_STATE]
        c_g = xbc_ref[:, D_INNER + SSD_GROUPS * D_STATE + g * D_STATE:
                      D_INNER + SSD_GROUPS * D_STATE + (g + 1) * D_STATE]
        st = state_scr[g]
        y_off = jnp.dot(c_g, st.astype(BF16), preferred_element_type=F32)
        xw_tiles, dec_tiles = [], []
        for pr in range(2):
            tile = g * 2 + pr
            x_pair = xbc_ref[:, tile * LANES:(tile + 1) * LANES].astype(F32)
            e_cols, w_cols, decs = [], [], []
            for hh in range(2):
                hb = SSD_HEADS + g * 4 + pr * 2 + hh
                ecs_col = ecs[:, hb:hb + 1]
                e_cols.append(jnp.exp(tot[:, hb:hb + 1] - ecs_col))
                w_cols.append(dt[:, hb:hb + 1] * jnp.exp(ecs_col))
                decs.append(jnp.exp(tot[:, hb:hb + 1]))
            sl = slice(tile * LANES, (tile + 1) * LANES)
            y_pair = (yf_ref[:, sl] + _lane_pair(e_cols[0], e_cols[1], lo_half) * y_off[:, pr * LANES:(pr + 1) * LANES]
                      + dskip_ref[:, sl] * x_pair)
            z_pair = jnp.where(c == 0, z_meta_ref[:, sl], z_tok_ref[:, sl]).astype(F32)
            y_scr[:, sl] = y_pair * (z_pair * _sigmoid(z_pair))
            xw_tiles.append((x_pair * _lane_pair(w_cols[0], w_cols[1], lo_half)).astype(BF16))
            dec_tiles.append(_lane_pair(decs[0], decs[1], lo_half[0:1, :]))
        xw = jnp.concatenate(xw_tiles, axis=1)
        dec = jnp.concatenate(dec_tiles, axis=1)
        upd = lax.dot_general(b_g, xw, (((0,), (0,)), ((), ())), preferred_element_type=F32)
        state_scr[g] = st * dec + upd

    gw = D_INNER // SSD_GROUPS
    for g in range(SSD_GROUPS):
        yg = y_scr[:, g * gw:(g + 1) * gw]
        yg = yg * lax.rsqrt(jnp.mean(yg * yg, axis=-1, keepdims=True) + EPS) * ng_ref[:, g * gw:(g + 1) * gw]
        y_scr[:, g * gw:(g + 1) * gw] = yg

    @pl.when(c > 0)
    def _():
        y_tok_ref[...] = y_scr[...].astype(BF16)

    @pl.when(c == 0)
    def _():
        y_meta_ref[...] = y_scr[...].astype(BF16)


def _ssd(xbc, dt_tok, dt_meta, proj_tok, proj_meta, dtb, alog, dskip, ng, bsz, T):
    nct = T // CHUNK
    nc = nct + 1
    small = lambda b, c: (0, 0)
    state = pltpu.VMEM((SSD_GROUPS, D_STATE, 4 * SSD_HEADDIM), F32)
    y_f = pl.pallas_call(
        _ssd_fwd_kernel,
        grid=(bsz, nc),
        in_specs=[
            pl.BlockSpec((CHUNK, D_XBC), lambda b, c: (b * nc + c, 0)),
            pl.BlockSpec((CHUNK, LANES), lambda b, c: (b * nct + jnp.maximum(c - 1, 0), 0)),
            pl.BlockSpec((CHUNK, LANES), small),
            pl.BlockSpec((1, LANES), small),
            pl.BlockSpec((1, LANES), small),
        ],
        out_specs=pl.BlockSpec((CHUNK, D_INNER), lambda b, c: (b * nc + c, 0)),
        out_shape=jax.ShapeDtypeStruct((bsz * nc * CHUNK, D_INNER), F32),
        scratch_shapes=[state],
        compiler_params=_cparams(("parallel", "arbitrary")),
        name="ssd_fwd",
    )(xbc, dt_tok, dt_meta, dtb, alog)

    zb = COL_Z // D_INNER
    rev = lambda b, i: nc - 1 - i
    tok_blk = lambda b, i: b * nct + jnp.maximum(rev(b, i) - 1, 0)
    y_tok, y_meta = pl.pallas_call(
        functools.partial(_ssd_bwd_kernel, nc=nc),
        grid=(bsz, nc),
        in_specs=[
            pl.BlockSpec((CHUNK, D_XBC), lambda b, i: (b * nc + rev(b, i), 0)),
            pl.BlockSpec((CHUNK, LANES), lambda b, i: (tok_blk(b, i), 0)),
            pl.BlockSpec((CHUNK, LANES), small),
            pl.BlockSpec((1, LANES), small),
            pl.BlockSpec((1, LANES), small),
            pl.BlockSpec((CHUNK, D_INNER), lambda b, i: (tok_blk(b, i), zb)),
            pl.BlockSpec((CHUNK, D_INNER), lambda b, i: (0, zb)),
            pl.BlockSpec((CHUNK, D_INNER), lambda b, i: (b * nc + rev(b, i), 0)),
            pl.BlockSpec((1, D_INNER), small),
            pl.BlockSpec((1, D_INNER), small),
        ],
        out_specs=[
            pl.BlockSpec((CHUNK, D_INNER), lambda b, i: (tok_blk(b, i), 0)),
            pl.BlockSpec((CHUNK, D_INNER), lambda b, i: (b, 0)),
        ],
        out_shape=[
            jax.ShapeDtypeStruct((bsz * T, D_INNER), BF16),
            jax.ShapeDtypeStruct((bsz * CHUNK, D_INNER), BF16),
        ],
        scratch_shapes=[state, pltpu.VMEM((CHUNK, D_INNER), F32)],
        compiler_params=_cparams(("parallel", "arbitrary")),
        name="ssd_bwd",
    )(xbc, dt_tok, dt_meta, dtb, alog, proj_tok, proj_meta, y_f, dskip, ng)
    return y_tok, y_meta


def _head_rmsnorm(x, gain, lo_half):
    sq = x * x
    s_lo = jnp.sum(jnp.where(lo_half, sq, 0.0), axis=-1, keepdims=True)
    s_all = jnp.sum(sq, axis=-1, keepdims=True)
    r_lo = lax.rsqrt(s_lo * (1.0 / NA_HEADDIM) + EPS)
    r_hi = lax.rsqrt((s_all - s_lo) * (1.0 / NA_HEADDIM) + EPS)
    return x * jnp.where(lo_half, r_lo, r_hi) * gain


def _qk_norm_kernel(q_ref, k_ref, qg_ref, kg_ref, qn_ref, kn_ref):
    lo_half = lax.broadcasted_iota(jnp.int32, (1, LANES), 1) < NA_HEADDIM
    for t in range(HEAD_PAIRS_NA):
        cols = slice(t * LANES, (t + 1) * LANES)
        qn = _head_rmsnorm(q_ref[:, cols].astype(F32), qg_ref[...], lo_half) * (NA_HEADDIM ** -0.5)
        qn_ref[:, cols] = qn.astype(BF16)
        kn_ref[:, cols] = _head_rmsnorm(k_ref[:, cols].astype(F32), kg_ref[...], lo_half).astype(BF16)


def _qk_norm(proj, qg, kg, tm):
    rows = proj.shape[0]
    small = lambda i: (0, 0)
    return pl.pallas_call(
        _qk_norm_kernel,
        grid=(rows // tm,),
        in_specs=[
            pl.BlockSpec((tm, D_NA), lambda i: (i, COL_Q // D_NA)),
            pl.BlockSpec((tm, D_NA), lambda i: (i, COL_K // D_NA)),
            pl.BlockSpec((1, LANES), small),
            pl.BlockSpec((1, LANES), small),
        ],
        out_specs=[pl.BlockSpec((tm, D_NA), lambda i: (i, 0)), pl.BlockSpec((tm, D_NA), lambda i: (i, 0))],
        out_shape=[jax.ShapeDtypeStruct((rows, D_NA), BF16), jax.ShapeDtypeStruct((rows, D_NA), BF16)],
        compiler_params=_cparams(("parallel",)),
        name="qk_norm",
    )(proj, proj, qg, kg)


def _na_kernel(q_ref, kn_ref, vn_ref, km_ref, vm_ref, tb_ref, mb_ref, out_ref,
               k_scr, v_scr, s_scr, p_scr, l_scr, *, grid_rows):
    j = pl.program_id(2)
    lo_half = lax.broadcasted_iota(jnp.int32, (1, LANES), 1) < NA_HEADDIM

    @pl.when(j == 0)
    def _():
        for scr in (k_scr, v_scr):
            scr[0:2 * NA_BLOCK, :] = jnp.zeros((2 * NA_BLOCK, LANES), BF16)

    @pl.when(j > 0)
    def _():
        for scr in (k_scr, v_scr):
            scr[0:NA_BLOCK, :] = scr[NA_BLOCK:2 * NA_BLOCK, :]
            scr[NA_BLOCK:2 * NA_BLOCK, :] = scr[2 * NA_BLOCK:3 * NA_BLOCK, :]

    k_scr[2 * NA_BLOCK:, :] = kn_ref[...]
    v_scr[2 * NA_BLOCK:, :] = vn_ref[...]

    @pl.when(j > 0)
    def _():
        jq = j - 1
        nt = (((1,), (1,)), ((), ()))
        n_keys = WIN_H * GRID_W
        offs = []
        for a in range(NA_ROWS_PER_BLOCK):
            r = NA_ROWS_PER_BLOCK * jq + a
            rstart = jnp.clip(r - WIN_H // 2, 0, grid_rows - WIN_H)
            off = pl.multiple_of((rstart - NA_ROWS_PER_BLOCK * (jq - 1)) * GRID_W, GRID_W)
            offs.append(off)
            q_r = q_ref[a * GRID_W:(a + 1) * GRID_W, :]
            zero = jnp.zeros_like(q_r)
            q2 = jnp.concatenate([jnp.where(lo_half, q_r, zero), jnp.where(lo_half, zero, q_r)], axis=0)
            s_scr[a, :, 0:n_keys] = (
                lax.dot_general(q2, k_scr[pl.ds(off, n_keys), :], nt, preferred_element_type=F32)
                + tb_ref[0, rstart - r + (WIN_H - 1)])
            s_scr[a, :, n_keys:] = lax.dot_general(q2, km_ref[...], nt, preferred_element_type=F32) + mb_ref[0]
        for a in range(NA_ROWS_PER_BLOCK):
            s = s_scr[a]
            p = jnp.exp(s - jnp.max(s, axis=-1, keepdims=True))
            l_scr[a] = jnp.sum(p, axis=-1, keepdims=True)
            p_scr[a] = p.astype(BF16)
        for a in range(NA_ROWS_PER_BLOCK):
            o = (jnp.dot(p_scr[a, :, 0:n_keys], v_scr[pl.ds(offs[a], n_keys), :], preferred_element_type=F32)
                 + jnp.dot(p_scr[a, :, n_keys:], vm_ref[...], preferred_element_type=F32)) / l_scr[a]
            out_ref[a * GRID_W:(a + 1) * GRID_W, :] = jnp.where(lo_half, o[0:GRID_W], o[GRID_W:]).astype(BF16)


def _na_meta_kernel(q_ref, k_ref, v_ref, mb_ref, out_ref):
    lo_half = lax.broadcasted_iota(jnp.int32, (1, LANES), 1) < NA_HEADDIM
    k_meta = k_ref[META_PAD:, :]
    v_meta = v_ref[META_PAD:, :]
    q = q_ref[...]
    nt = (((1,), (1,)), ((), ()))
    outs = []
    for hh in range(2):
        q_h = jnp.where(lo_half == (hh == 0), q, jnp.zeros_like(q))
        s = lax.dot_general(q_h, k_meta, nt, preferred_element_type=F32) + mb_ref[0, hh:hh + 1, 0:N_META]
        p = jnp.exp(s - jnp.max(s, axis=-1, keepdims=True))
        o = jnp.dot(p.astype(BF16), v_meta, preferred_element_type=F32) / jnp.sum(p, axis=-1, keepdims=True)
        outs.append(o)
    out_ref[...] = jnp.where(lo_half, outs[0], outs[1]).astype(BF16)


def _na(qn, kn, proj_tok, kn_meta, proj_meta, tb, mb2, bsz, T):
    nb = T // NA_BLOCK
    vc = COL_V // LANES
    blk = (NA_BLOCK, LANES)
    q_blk = lambda p, b, j: (b * nb + jnp.maximum(j - 1, 0), p)
    kv_blk = lambda col: (lambda p, b, j: (b * nb + jnp.minimum(j, nb - 1), col + p))
    two_heads = 2 * GRID_W
    n_keys = WIN_H * GRID_W + CHUNK
    return pl.pallas_call(
        functools.partial(_na_kernel, grid_rows=T // GRID_W),
        grid=(HEAD_PAIRS_NA, bsz, nb + 1),
        in_specs=[
            pl.BlockSpec(blk, q_blk),
            pl.BlockSpec(blk, kv_blk(0)),
            pl.BlockSpec(blk, kv_blk(vc)),
            pl.BlockSpec((CHUNK, LANES), lambda p, b, j: (0, p)),
            pl.BlockSpec((CHUNK, LANES), lambda p, b, j: (0, vc + p)),
            pl.BlockSpec((1, WIN_H, two_heads, WIN_H * GRID_W), lambda p, b, j: (p, 0, 0, 0)),
            pl.BlockSpec((1, two_heads, LANES), lambda p, b, j: (p, 0, 0)),
        ],
        out_specs=pl.BlockSpec(blk, q_blk),
        out_shape=jax.ShapeDtypeStruct((bsz * T, D_NA), BF16),
        scratch_shapes=[
            pltpu.VMEM((3 * NA_BLOCK, LANES), BF16),
            pltpu.VMEM((3 * NA_BLOCK, LANES), BF16),
            pltpu.VMEM((NA_ROWS_PER_BLOCK, two_heads, n_keys), F32),
            pltpu.VMEM((NA_ROWS_PER_BLOCK, two_heads, n_keys), BF16),
            pltpu.VMEM((NA_ROWS_PER_BLOCK, two_heads, 1), F32),
        ],
        compiler_params=_cparams(("arbitrary", "arbitrary", "arbitrary")),
        name="natten",
    )(qn, kn, proj_tok, kn_meta, proj_meta, tb, mb2)


def _na_meta(qn_meta, kn_meta, proj_meta, mb):
    vc = COL_V // LANES
    blk = (CHUNK, LANES)
    return pl.pallas_call(
        _na_meta_kernel,
        grid=(HEAD_PAIRS_NA,),
        in_specs=[
            pl.BlockSpec(blk, lambda p: (0, p)),
            pl.BlockSpec(blk, lambda p: (0, p)),
            pl.BlockSpec(blk, lambda p: (0, vc + p)),
            pl.BlockSpec((1, 8, LANES), lambda p: (p, 0, 0)),
        ],
        out_specs=pl.BlockSpec(blk, lambda p: (0, p)),
        out_shape=jax.ShapeDtypeStruct((CHUNK, D_NA), BF16),
        compiler_params=_cparams(("arbitrary",)),
        name="natten_meta",
    )(qn_meta, kn_meta, proj_meta, mb)


def _rel_bias_table(rel_bias):
    w = np.arange(GRID_W)
    cstart = np.clip(w - WIN_W // 2, 0, GRID_W - WIN_W)
    cc = np.arange(GRID_W)
    inside = (cc[None, :] >= cstart[:, None]) & (cc[None, :] < cstart[:, None] + WIN_W)
    t = np.arange(2 * WIN_W - 1)
    onehot = ((cc[None, None, :] - w[None, :, None] + (WIN_W - 1)) == t[:, None, None]) & inside[None]
    t15 = jnp.einsum("hrt,twc->hrwc", rel_bias, jnp.asarray(onehot, F32), precision=lax.Precision.HIGHEST)
    t15 = jnp.where(inside[None, None], t15, NEG_BIG)
    per_d0 = [t15[:, d0:d0 + WIN_H].transpose(0, 2, 1, 3).reshape(NA_HEADS, GRID_W, WIN_H * GRID_W)
              for d0 in range(WIN_H)]
    g = jnp.stack(per_d0, axis=1)
    g = g.reshape(HEAD_PAIRS_NA, 2, WIN_H, GRID_W, WIN_H * GRID_W).transpose(0, 2, 1, 3, 4)
    return g.reshape(HEAD_PAIRS_NA, WIN_H, 2 * GRID_W, WIN_H * GRID_W)


def _merge_kernel(ys_ref, yn_ref, gs_ref, gn_ref, h_ref, wbs_ref, wbn_ref, wo_ref, out_ref):
    m_s = jnp.dot(ys_ref[...], wbs_ref[...], preferred_element_type=F32)
    m_n = jnp.dot(yn_ref[...], wbn_ref[...], preferred_element_type=F32)
    merged = _sigmoid(gs_ref[...].astype(F32)) * m_s + _sigmoid(gn_ref[...].astype(F32)) * m_n
    out_ref[...] = h_ref[...] + jnp.dot(merged.astype(BF16), wo_ref[...], preferred_element_type=F32)


def _merge(y_ssd, y_na, proj, h, wbs, wbn, wo, tm, shared_rows):
    rows = y_ssd.shape[0]
    gsc, gnc = COL_GS // D_MODEL, COL_GN // D_MODEL
    r = (lambda i: 0) if shared_rows else (lambda i: i)
    full = lambda i: (0, 0)
    return pl.pallas_call(
        _merge_kernel,
        grid=(rows // tm,),
        in_specs=[
            pl.BlockSpec((tm, D_INNER), lambda i: (i, 0)),
            pl.BlockSpec((tm, D_NA), lambda i: (r(i), 0)),
            pl.BlockSpec((tm, D_MODEL), lambda i: (r(i), gsc)),
            pl.BlockSpec((tm, D_MODEL), lambda i: (r(i), gnc)),
            pl.BlockSpec((tm, D_MODEL), lambda i: (r(i), 0)),
            pl.BlockSpec((D_INNER, D_MODEL), full),
            pl.BlockSpec((D_NA, D_MODEL), full),
            pl.BlockSpec((D_MODEL, D_MODEL), full),
        ],
        out_specs=pl.BlockSpec((tm, D_MODEL), lambda i: (i, 0)),
        out_shape=jax.ShapeDtypeStruct((rows, D_MODEL), F32),
        compiler_params=_cparams(("parallel",)),
        name="merge",
    )(y_ssd, y_na, proj, proj, h, wbs, wbn, wo)


FFN_HALO = 16
FFN_TF = 256


def _ffn_kernel(h_ref, prev_ref, next_ref, meta_ref, g_ref, wup_ref, cw_ref, cb_ref, wd_ref, out_ref,
                u_scr, act_scr, *, tiles_per_seq, tm):
    pos = pl.program_id(0) % tiles_per_seq
    gain = g_ref[...]
    prev = jnp.where(pos == 0, meta_ref[...], prev_ref[...])
    nxt = jnp.where(pos == tiles_per_seq - 1, 0.0, next_ref[...])
    u_scr[0:FFN_HALO, :] = _rmsnorm_rows(prev, gain).astype(BF16)
    u_scr[FFN_HALO:FFN_HALO + tm, :] = _rmsnorm_rows(h_ref[...], gain).astype(BF16)
    u_scr[FFN_HALO + tm:, :] = _rmsnorm_rows(nxt, gain).astype(BF16)
    u = u_scr[...]

    n_ext = tm + 2 * FFN_HALO

    def conv(x_ext, cols):
        y = cw_ref[1:2, cols] * x_ext + cb_ref[:, cols]
        y = y + pltpu.roll(cw_ref[0:1, cols] * x_ext, 1, 0)
        y = y + pltpu.roll(cw_ref[2:3, cols] * x_ext, n_ext - 1, 0)
        return y[FFN_HALO:FFN_HALO + tm]

    for k in range(D_FF // FFN_TF):
        cols_a = slice(k * FFN_TF, (k + 1) * FFN_TF)
        cols_g = slice(D_FF + k * FFN_TF, D_FF + (k + 1) * FFN_TF)
        a = conv(jnp.dot(u, wup_ref[:, cols_a], preferred_element_type=F32), cols_a)
        g = conv(jnp.dot(u, wup_ref[:, cols_g], preferred_element_type=F32), cols_g)
        act_scr[:, cols_a] = (g * _sigmoid(g) * a).astype(BF16)
    out_ref[...] = h_ref[...] + jnp.dot(act_scr[...], wd_ref[...], preferred_element_type=F32)


def _ffn(h1, h1_meta, g_ffn, w_up, cw, cb, w_down, bsz, T, tm):
    rows = bsz * T
    tps = T // tm
    hb = tm // FFN_HALO
    mb = CHUNK // FFN_HALO
    nblk = rows // FFN_HALO
    const = lambda i: (0, 0)
    resident = pl.Buffered(1)
    return pl.pallas_call(
        functools.partial(_ffn_kernel, tiles_per_seq=tps, tm=tm),
        grid=(rows // tm,),
        in_specs=[
            pl.BlockSpec((tm, D_MODEL), lambda i: (i, 0)),
            pl.BlockSpec((FFN_HALO, D_MODEL), lambda i: (jnp.maximum(i * hb - 1, 0), 0)),
            pl.BlockSpec((FFN_HALO, D_MODEL), lambda i: (jnp.minimum((i + 1) * hb, nblk - 1), 0)),
            pl.BlockSpec((FFN_HALO, D_MODEL), lambda i: ((i // tps) * mb + mb - 1, 0)),
            pl.BlockSpec((1, D_MODEL), const),
            pl.BlockSpec((D_MODEL, 2 * D_FF), const, pipeline_mode=resident),
            pl.BlockSpec((8, 2 * D_FF), const),
            pl.BlockSpec((1, 2 * D_FF), const),
            pl.BlockSpec((D_FF, D_MODEL), const, pipeline_mode=resident),
        ],
        out_specs=pl.BlockSpec((tm, D_MODEL), lambda i: (i, 0)),
        out_shape=jax.ShapeDtypeStruct((rows, D_MODEL), F32),
        scratch_shapes=[
            pltpu.VMEM((tm + 2 * FFN_HALO, D_MODEL), BF16),
            pltpu.VMEM((tm, D_FF), BF16),
        ],
        compiler_params=_cparams(("parallel",)),
        name="ffn",
    )(h1, h1, h1, h1_meta, g_ffn, w_up, cw, cb, w_down)


def _pad_rows(a, n):
    return jnp.pad(a, ((0, n - a.shape[0]), (0, 0)))


def _row_tile(rows, cap):
    tm = min(cap, rows)
    assert rows % tm == 0
    return tm


def kernel(x_prompt, x_sample, meta_tokens, g_mix, w_in, ssd_conv_w, ssd_conv_b, dt_bias_f, dt_bias_b,
           a_log_f, a_log_b, d_skip, ssd_norm_g, q_norm_g, k_norm_g, rel_bias, meta_bias, w_br_ssd,
           w_br_na, w_out, g_ffn, w_up, ffn_conv_w, ffn_conv_b, w_down):
    assert g_mix.shape[0] == 1, "single-layer block"
    w = w_in[0]
    o_z, o_xbc = 0, D_INNER
    o_dtf = o_xbc + D_XBC
    o_q = o_dtf + 2 * SSD_HEADS
    w_main = jnp.concatenate([w[:, o_xbc:o_dtf], w[:, o_z:o_xbc], w[:, o_q:]], axis=1).astype(BF16)
    w_dt = jnp.pad(w[:, o_dtf:o_q], ((0, 0), (0, LANES - 2 * SSD_HEADS))).astype(BF16)
    g_mix2 = g_mix.astype(F32)
    lane_pad = lambda v: jnp.pad(v, (0, LANES - v.shape[0]))[None, :].astype(F32)
    dtb = lane_pad(jnp.concatenate([dt_bias_f[0], dt_bias_b[0]]))
    alog = lane_pad(jnp.concatenate([a_log_f[0], a_log_b[0]]))
    dskip = jnp.repeat(d_skip[0].astype(F32), SSD_HEADDIM)[None, :]
    ng = ssd_norm_g.astype(F32)
    conv_w = _pad_rows(ssd_conv_w[0].astype(F32), 8)
    conv_b = ssd_conv_b.astype(F32)
    qg = jnp.tile(q_norm_g[0].astype(F32), 2)[None, :]
    kg = jnp.tile(k_norm_g[0].astype(F32), 2)[None, :]
    tb = _rel_bias_table(rel_bias[0].astype(F32))
    mb = jnp.pad(meta_bias[0].astype(F32).reshape(HEAD_PAIRS_NA, 2, N_META), ((0, 0), (0, 6), (0, LANES - N_META)))
    mb2 = jnp.repeat(meta_bias[0].astype(F32).reshape(HEAD_PAIRS_NA, 2, N_META), GRID_W, axis=1)
    mb2 = jnp.pad(mb2, ((0, 0), (0, 0), (META_PAD, 0)), constant_values=NEG_BIG)
    wbs, wbn, wo = w_br_ssd[0].astype(BF16), w_br_na[0].astype(BF16), w_out[0].astype(BF16)
    w_up_b, w_down_b = w_up[0].astype(BF16), w_down[0].astype(BF16)
    ffn_cw = _pad_rows(ffn_conv_w[0].astype(F32), 8)
    ffn_cb = ffn_conv_b.astype(F32)

    h_meta = jnp.concatenate([jnp.zeros((META_PAD, D_MODEL), F32), meta_tokens.astype(F32)], axis=0)
    proj_meta, dt_meta = _inproj(h_meta, g_mix2, w_main, w_dt, CHUNK)
    qn_meta, kn_meta = _qk_norm(proj_meta, qg, kg, CHUNK)
    y_na_meta = _na_meta(qn_meta, kn_meta, proj_meta, mb)

    def run(x):
        bsz, T, _ = x.shape
        assert T % NA_BLOCK == 0 and T % CHUNK == 0
        x2d = x.reshape(bsz * T, D_MODEL)
        tm = _row_tile(bsz * T, 1024)
        proj_tok, dt_tok = _inproj(x2d, g_mix2, w_main, w_dt, _row_tile(bsz * T, 2048))
        xbc = _ssd_conv(proj_tok, proj_meta, conv_w, conv_b, bsz, T)
        y_ssd, y_ssd_meta = _ssd(xbc, dt_tok, dt_meta, proj_tok, proj_meta, dtb, alog, dskip, ng, bsz, T)
        qn, kn = _qk_norm(proj_tok, qg, kg, tm)
        y_na = _na(qn, kn, proj_tok, kn_meta, proj_meta, tb, mb2, bsz, T)
        h1 = _merge(y_ssd, y_na, proj_tok, x2d, wbs, wbn, wo, tm, False)
        h1_meta = _merge(y_ssd_meta, y_na_meta, proj_meta, h_meta, wbs, wbn, wo, CHUNK, True)
        tf = _row_tile(T, 1024)
        out = _ffn(h1, h1_meta, g_ffn.astype(F32), w_up_b, ffn_cw, ffn_cb, w_down_b, bsz, T, tf)
        return out.reshape(bsz, T, D_MODEL)

    return (run(x_prompt), run(x_sample))
```

```python
import functools

import numpy as np
import jax
import jax.numpy as jnp
from jax import lax
from jax.experimental import pallas as pl
from jax.experimental.pallas import tpu as pltpu

F32 = jnp.float32
BF16 = jnp.bfloat16

D_MODEL = 1024
N_META = 16
GRID_W = 64
D_INNER = 2048
SSD_HEADDIM = 64
SSD_HEADS = 32
SSD_GROUPS = 8
D_STATE = 128
D_CONV = 5
CHUNK = 128
D_XBC = D_INNER + 2 * SSD_GROUPS * D_STATE
NA_HEADS = 16
NA_HEADDIM = 64
D_NA = NA_HEADS * NA_HEADDIM
WIN_H = 8
WIN_W = 16
D_FF = 2816
FFN_CONV = 3
EPS = 1e-6

LANES = 128
META_PAD = CHUNK - N_META
HEAD_PAIRS_SSD = SSD_HEADS // 2
HEAD_PAIRS_NA = NA_HEADS // 2
NA_ROWS_PER_BLOCK = 16
NA_BLOCK = NA_ROWS_PER_BLOCK * GRID_W
NEG_BIG = -1e30

COL_XBC = 0
COL_Z = COL_XBC + D_XBC
COL_Q = COL_Z + D_INNER
COL_K = COL_Q + D_NA
COL_V = COL_K + D_NA
COL_GS = COL_V + D_NA
COL_GN = COL_GS + D_MODEL
D_PROJ = COL_GN + D_MODEL

VMEM_LIMIT = 56 * 1024 * 1024


def _cparams(sem):
    return pltpu.CompilerParams(dimension_semantics=sem, vmem_limit_bytes=VMEM_LIMIT)


def _sigmoid(x):
    return 1.0 / (1.0 + jnp.exp(-x))


def _softplus(x):
    return jnp.maximum(x, 0.0) + jnp.log(1.0 + jnp.exp(-jnp.abs(x)))


def _rmsnorm_rows(x, g):
    ms = jnp.mean(x * x, axis=-1, keepdims=True)
    return x * lax.rsqrt(ms + EPS) * g


def _inproj_kernel(x_ref, g_ref, w_ref, wdt_ref, out_ref, dt_ref, u_scr):
    @pl.when(pl.program_id(1) == 0)
    def _():
        u = _rmsnorm_rows(x_ref[...], g_ref[...]).astype(BF16)
        u_scr[...] = u
        dt_ref[...] = jnp.dot(u, wdt_ref[...], preferred_element_type=F32)

    out_ref[...] = jnp.dot(u_scr[...], w_ref[...], preferred_element_type=F32).astype(BF16)


def _inproj(x2d, g_mix, w_main, w_dt, tm):
    rows = x2d.shape[0]
    tn = 1024
    return pl.pallas_call(
        _inproj_kernel,
        grid=(rows // tm, D_PROJ // tn),
        in_specs=[
            pl.BlockSpec((tm, D_MODEL), lambda i, j: (i, 0)),
            pl.BlockSpec((1, D_MODEL), lambda i, j: (0, 0)),
            pl.BlockSpec((D_MODEL, tn), lambda i, j: (0, j)),
            pl.BlockSpec((D_MODEL, LANES), lambda i, j: (0, 0)),
        ],
        out_specs=[
            pl.BlockSpec((tm, tn), lambda i, j: (i, j)),
            pl.BlockSpec((tm, LANES), lambda i, j: (i, 0)),
        ],
        out_shape=[
            jax.ShapeDtypeStruct((rows, D_PROJ), BF16),
            jax.ShapeDtypeStruct((rows, LANES), F32),
        ],
        scratch_shapes=[pltpu.VMEM((tm, D_MODEL), BF16)],
        compiler_params=_cparams(("parallel", "arbitrary")),
        name="inproj",
    )(x2d, g_mix, w_main, w_dt)


HALO = 16


CONV_PAD = (D_CONV - 1) // 2
CONV_SIDE_TAPS = tuple(k for k in range(D_CONV) if k != CONV_PAD)


def _conv_shift_matrix():
    sh = np.zeros((len(CONV_SIDE_TAPS) * CHUNK, CHUNK + 2 * HALO), np.float32)
    r = np.arange(CHUNK)
    for j, k in enumerate(CONV_SIDE_TAPS):
        sh[j * CHUNK + r, HALO + r + k - CONV_PAD] = 1.0
    return jnp.asarray(sh, BF16)


def _conv_silu_chunk(c, nc, tok_ref, prev_ref, next_ref, meta_ref, shift_ref, w_ref, b_ref, out_ref):
    is_meta = c == 0
    strip = 2 * LANES
    for s in range(D_XBC // strip):
        cols = slice(s * strip, (s + 1) * strip)
        main = jnp.where(is_meta, meta_ref[:, cols], tok_ref[:, cols])
        prev = jnp.where(c == 1, meta_ref[CHUNK - HALO:, cols], prev_ref[:, cols])
        prev = jnp.where(is_meta, jnp.zeros_like(prev), prev)
        nxt = jnp.where(c == nc - 1, jnp.zeros_like(prev), next_ref[:, cols])
        ext = jnp.concatenate([prev, main, nxt], axis=0)
        acc = w_ref[CONV_PAD:CONV_PAD + 1, cols] * main.astype(F32) + b_ref[:, cols]
        for j, k in enumerate(CONV_SIDE_TAPS):
            shifted = jnp.dot(shift_ref[j * CHUNK:(j + 1) * CHUNK, :], ext, preferred_element_type=F32)
            acc = acc + w_ref[k:k + 1, cols] * shifted
        out_ref[:, cols] = (acc * _sigmoid(acc)).astype(BF16)


DK_DT, DK_CS, DK_ECS, DK_CS_T, DK_ECS_T, DK_DT_T = range(6)
DK_N = 6
DT_PREP_CHUNKS = 8


def _dt_prep_kernel(dt_ref, dtb_ref, alog_ref, out_ref, *, n_chunks, zero_pad_rows):
    row = lax.broadcasted_iota(jnp.int32, (CHUNK, LANES), 0)
    col = lax.broadcasted_iota(jnp.int32, (CHUNK, LANES), 1)
    tri = (col <= row).astype(F32)
    neg_a = -jnp.exp(alog_ref[...])
    for i in range(n_chunks):
        dt = _softplus(dt_ref[i * CHUNK:(i + 1) * CHUNK, :] + dtb_ref[...])
        if zero_pad_rows:
            dt = jnp.where(row < META_PAD, 0.0, dt)
        a = dt * neg_a
        cs = jnp.dot(tri, a, preferred_element_type=F32, precision=lax.Precision.HIGHEST)
        ecs = cs - a
        out_ref[i, DK_DT] = dt
        out_ref[i, DK_CS] = cs
        out_ref[i, DK_ECS] = ecs
        out_ref[i, DK_CS_T] = cs.T
        out_ref[i, DK_ECS_T] = ecs.T
        out_ref[i, DK_DT_T] = dt.T


def _dt_prep(dt_raw, dtb, alog, chunks_per_step, zero_pad_rows):
    n = dt_raw.shape[0] // CHUNK
    assert n % chunks_per_step == 0
    small = lambda i: (0, 0)
    return pl.pallas_call(
        functools.partial(_dt_prep_kernel, n_chunks=chunks_per_step, zero_pad_rows=zero_pad_rows),
        grid=(n // chunks_per_step,),
        in_specs=[
            pl.BlockSpec((chunks_per_step * CHUNK, LANES), lambda i: (i, 0)),
            pl.BlockSpec((1, LANES), small),
            pl.BlockSpec((1, LANES), small),
        ],
        out_specs=pl.BlockSpec((chunks_per_step, DK_N, CHUNK, LANES), lambda i: (i, 0, 0, 0)),
        out_shape=jax.ShapeDtypeStruct((n, DK_N, CHUNK, LANES), F32),
        compiler_params=_cparams(("parallel",)),
        name="dt_prep",
    )(dt_raw, dtb, alog)


def _lane_pair(col_lo, col_hi, lo_mask):
    return jnp.where(lo_mask, col_lo, col_hi)


def _decay_tiles(c, dk_tok_ref, dk_meta_ref):
    return lambda k: jnp.where(c == 0, dk_meta_ref[0, k], dk_tok_ref[0, k])


def _ssd_fwd_kernel(tok_ref, prev_ref, next_ref, meta_ref, shift_ref, w_ref, b_ref, dk_tok_ref, dk_meta_ref,
                    xbc_ref, y_ref, state_scr, *, nc):
    c = pl.program_id(1)

    @pl.when(c == 0)
    def _():
        state_scr[...] = jnp.zeros_like(state_scr)

    _conv_silu_chunk(c, nc, tok_ref, prev_ref, next_ref, meta_ref, shift_ref, w_ref, b_ref, xbc_ref)

    dk = _decay_tiles(c, dk_tok_ref, dk_meta_ref)
    dt, cs, ecs, cs_t, ecs_t, dt_t = (dk(k) for k in (DK_DT, DK_CS, DK_ECS, DK_CS_T, DK_ECS_T, DK_DT_T))
    row = lax.broadcasted_iota(jnp.int32, (CHUNK, LANES), 0)
    col = lax.broadcasted_iota(jnp.int32, (CHUNK, LANES), 1)
    lower = row >= col
    strict_lower = row > col
    strict_upper = row < col
    lo_half = col < SSD_HEADDIM
    tot = cs[CHUNK - 1:CHUNK, :]
    heads_per_group = SSD_HEADS // SSD_GROUPS
    gw = heads_per_group * SSD_HEADDIM
    head_of_lane = lax.broadcasted_iota(jnp.int32, (CHUNK, gw), 1) // SSD_HEADDIM

    for g in range(SSD_GROUPS):
        b_g = xbc_ref[:, D_INNER + g * D_STATE:D_INNER + (g + 1) * D_STATE]
        c_g = xbc_ref[:, D_INNER + SSD_GROUPS * D_STATE + g * D_STATE:
                      D_INNER + SSD_GROUPS * D_STATE + (g + 1) * D_STATE]
        cb = lax.dot_general(c_g, b_g, (((1,), (1,)), ((), ())), preferred_element_type=F32)
        st = state_scr[g]
        y_off = jnp.dot(c_g, st.astype(BF16), preferred_element_type=F32)
        x_g = xbc_ref[:, g * gw:(g + 1) * gw]
        m_mats, x_blocks, e_cols, w_cols, decs = [], [], [], [], []
        for r in range(heads_per_group):
            h = g * heads_per_group + r
            hb = SSD_HEADS + h
            csf_col = cs[:, h:h + 1]
            e_mat = jnp.where(lower, csf_col - cs_t[h:h + 1, :], ecs_t[hb:hb + 1, :] - ecs[:, hb:hb + 1])
            dtf_row, dtb_row = dt_t[h:h + 1, :], dt_t[hb:hb + 1, :]
            dsel = jnp.where(strict_lower, dtf_row, jnp.where(strict_upper, dtb_row, dtf_row + dtb_row))
            m_mats.append((cb * jnp.exp(e_mat) * dsel).astype(BF16))
            x_blocks.append(jnp.where(head_of_lane == r, x_g, jnp.zeros_like(x_g)))
            e_cols.append(jnp.exp(csf_col))
            w_cols.append(dt[:, h:h + 1] * jnp.exp(tot[:, h:h + 1] - csf_col))
            decs.append(jnp.exp(tot[:, h:h + 1]))
        y_diag = jnp.dot(jnp.concatenate(m_mats, axis=1), jnp.concatenate(x_blocks, axis=0),
                         preferred_element_type=F32)
        xw_tiles, dec_tiles = [], []
        for pr in range(2):
            tile = g * 2 + pr
            sl = slice(pr * LANES, (pr + 1) * LANES)
            e_pair = _lane_pair(e_cols[2 * pr], e_cols[2 * pr + 1], lo_half)
            y_ref[:, tile * LANES:(tile + 1) * LANES] = y_diag[:, sl] + e_pair * y_off[:, sl]
            w_pair = _lane_pair(w_cols[2 * pr], w_cols[2 * pr + 1], lo_half)
            xw_tiles.append((x_g[:, sl].astype(F32) * w_pair).astype(BF16))
            dec_tiles.append(_lane_pair(decs[2 * pr], decs[2 * pr + 1], lo_half[0:1, :]))
        xw = jnp.concatenate(xw_tiles, axis=1)
        dec = jnp.concatenate(dec_tiles, axis=1)
        upd = lax.dot_general(b_g, xw, (((0,), (0,)), ((), ())), preferred_element_type=F32)
        state_scr[g] = st * dec + upd


def _ssd_bwd_kernel(xbc_ref, dk_tok_ref, dk_meta_ref, z_tok_ref, z_meta_ref, yf_ref,
                    dskip_ref, ng_ref, y_tok_ref, y_meta_ref, state_scr, y_scr, *, nc):
    i = pl.program_id(1)
    c = nc - 1 - i

    @pl.when(i == 0)
    def _():
        state_scr[...] = jnp.zeros_like(state_scr)

    dk = _decay_tiles(c, dk_tok_ref, dk_meta_ref)
    dt, ecs = dk(DK_DT), dk(DK_ECS)
    lo_half = lax.broadcasted_iota(jnp.int32, (CHUNK, LANES), 1) < SSD_HEADDIM
    tot = dk(DK_CS)[CHUNK - 1:CHUNK, :]

    for g in range(SSD_GROUPS):
        b_g = xbc_ref[:, D_INNER + g * D_STATE:D_INNER + (g + 1) * D_STATE]
        c_g = xbc_ref[:, D_INNER + SSD_GROUPS * D_STATE + g * D_STATE:
                      D_INNER + SSD_GROUPS * D_STATE + (g + 1) * D_STATE]
        st = state_scr[g]
        y_off = jnp.dot(c_g, st.astype(BF16), preferred_element_type=F32)
        xw_tiles, dec_tiles = [], []
        for pr in range(2):
            tile = g * 2 + pr
            x_pair = xbc_ref[:, tile * LANES:(tile + 1) * LANES].astype(F32)
            e_cols, w_cols, decs = [], [], []
            for hh in range(2):
                hb = SSD_HEADS + g * 4 + pr * 2 + hh
                ecs_col = ecs[:, hb:hb + 1]
                e_cols.append(jnp.exp(tot[:, hb:hb + 1] - ecs_col))
                w_cols.append(dt[:, hb:hb + 1] * jnp.exp(ecs_col))
                decs.append(jnp.exp(tot[:, hb:hb + 1]))
            sl = slice(tile * LANES, (tile + 1) * LANES)
            y_pair = (yf_ref[:, sl] + _lane_pair(e_cols[0], e_cols[1], lo_half) * y_off[:, pr * LANES:(pr + 1) * LANES]
                      + dskip_ref[:, sl] * x_pair)
            z_pair = jnp.where(c == 0, z_meta_ref[:, sl], z_tok_ref[:, sl]).astype(F32)
            y_scr[:, sl] = y_pair * (z_pair * _sigmoid(z_pair))
            xw_tiles.append((x_pair * _lane_pair(w_cols[0], w_cols[1], lo_half)).astype(BF16))
            dec_tiles.append(_lane_pair(decs[0], decs[1], lo_half[0:1, :]))
        xw = jnp.concatenate(xw_tiles, axis=1)
        dec = jnp.concatenate(dec_tiles, axis=1)
        upd = lax.dot_general(b_g, xw, (((0,), (0,)), ((), ())), preferred_element_type=F32)
        state_scr[g] = st * dec + upd

    gw = D_INNER // SSD_GROUPS
    for g in range(SSD_GROUPS):
        yg = y_scr[:, g * gw:(g + 1) * gw]
        yg = yg * lax.rsqrt(jnp.mean(yg * yg, axis=-1, keepdims=True) + EPS) * ng_ref[:, g * gw:(g + 1) * gw]
        y_scr[:, g * gw:(g + 1) * gw] = yg

    @pl.when(c > 0)
    def _():
        y_tok_ref[...] = y_scr[...].astype(BF16)

    @pl.when(c == 0)
    def _():
        y_meta_ref[...] = y_scr[...].astype(BF16)


def _ssd(dk_tok, dk_meta, proj_tok, proj_meta, conv_w, conv_b, dskip, ng, bsz, T):
    nct = T // CHUNK
    nc = nct + 1
    hb = CHUNK // HALO
    small = lambda b, c: (0, 0)
    dk_blk = (1, DK_N, CHUNK, LANES)
    dk_first = lambda b, c: (0, 0, 0, 0)
    state = pltpu.VMEM((SSD_GROUPS, D_STATE, 4 * SSD_HEADDIM), F32)
    shift = _conv_shift_matrix()
    xbc, y_f = pl.pallas_call(
        functools.partial(_ssd_fwd_kernel, nc=nc),
        grid=(bsz, nc),
        in_specs=[
            pl.BlockSpec((CHUNK, D_XBC), lambda b, c: (b * nct + jnp.maximum(c - 1, 0), 0)),
            pl.BlockSpec((HALO, D_XBC), lambda b, c: (jnp.maximum(b * nct * hb + (c - 1) * hb - 1, 0), 0)),
            pl.BlockSpec((HALO, D_XBC), lambda b, c: (b * nct * hb + jnp.minimum(c, nct - 1) * hb, 0)),
            pl.BlockSpec((CHUNK, D_XBC), small),
            pl.BlockSpec(shift.shape, small),
            pl.BlockSpec((8, D_XBC), small),
            pl.BlockSpec((1, D_XBC), small),
            pl.BlockSpec(dk_blk, lambda b, c: (b * nct + jnp.maximum(c - 1, 0), 0, 0, 0)),
            pl.BlockSpec(dk_blk, dk_first),
        ],
        out_specs=[
            pl.BlockSpec((CHUNK, D_XBC), lambda b, c: (b * nc + c, 0)),
            pl.BlockSpec((CHUNK, D_INNER), lambda b, c: (b * nc + c, 0)),
        ],
        out_shape=[
            jax.ShapeDtypeStruct((bsz * nc * CHUNK, D_XBC), BF16),
            jax.ShapeDtypeStruct((bsz * nc * CHUNK, D_INNER), F32),
        ],
        scratch_shapes=[state],
        compiler_params=_cparams(("parallel", "arbitrary")),
        name="ssd_fwd",
    )(proj_tok, proj_tok, proj_tok, proj_meta, shift, conv_w, conv_b, dk_tok, dk_meta)

    zb = COL_Z // D_INNER
    rev = lambda b, i: nc - 1 - i
    tok_blk = lambda b, i: b * nct + jnp.maximum(rev(b, i) - 1, 0)
    y_tok, y_meta = pl.pallas_call(
        functools.partial(_ssd_bwd_kernel, nc=nc),
        grid=(bsz, nc),
        in_specs=[
            pl.BlockSpec((CHUNK, D_XBC), lambda b, i: (b * nc + rev(b, i), 0)),
            pl.BlockSpec(dk_blk, lambda b, i: (tok_blk(b, i), 0, 0, 0)),
            pl.BlockSpec(dk_blk, dk_first),
            pl.BlockSpec((CHUNK, D_INNER), lambda b, i: (tok_blk(b, i), zb)),
            pl.BlockSpec((CHUNK, D_INNER), lambda b, i: (0, zb)),
            pl.BlockSpec((CHUNK, D_INNER), lambda b, i: (b * nc + rev(b, i), 0)),
            pl.BlockSpec((1, D_INNER), small),
            pl.BlockSpec((1, D_INNER), small),
        ],
        out_specs=[
            pl.BlockSpec((CHUNK, D_INNER), lambda b, i: (tok_blk(b, i), 0)),
            pl.BlockSpec((CHUNK, D_INNER), lambda b, i: (b, 0)),
        ],
        out_shape=[
            jax.ShapeDtypeStruct((bsz * T, D_INNER), BF16),
            jax.ShapeDtypeStruct((bsz * CHUNK, D_INNER), BF16),
        ],
        scratch_shapes=[state, pltpu.VMEM((CHUNK, D_INNER), F32)],
        compiler_params=_cparams(("parallel", "arbitrary")),
        name="ssd_bwd",
    )(xbc, dk_tok, dk_meta, proj_tok, proj_meta, y_f, dskip, ng)
    return y_tok, y_meta


def _head_rmsnorm(x, gain, lo_half):
    sq = x * x
    s_lo = jnp.sum(jnp.where(lo_half, sq, 0.0), axis=-1, keepdims=True)
    s_all = jnp.sum(sq, axis=-1, keepdims=True)
    r_lo = lax.rsqrt(s_lo * (1.0 / NA_HEADDIM) + EPS)
    r_hi = lax.rsqrt((s_all - s_lo) * (1.0 / NA_HEADDIM) + EPS)
    return x * jnp.where(lo_half, r_lo, r_hi) * gain


def _qk_norm_kernel(q_ref, k_ref, qg_ref, kg_ref, qn_ref, kn_ref):
    lo_half = lax.broadcasted_iota(jnp.int32, (1, LANES), 1) < NA_HEADDIM
    for t in range(HEAD_PAIRS_NA):
        cols = slice(t * LANES, (t + 1) * LANES)
        qn = _head_rmsnorm(q_ref[:, cols].astype(F32), qg_ref[...], lo_half) * (NA_HEADDIM ** -0.5)
        qn_ref[:, cols] = qn.astype(BF16)
        kn_ref[:, cols] = _head_rmsnorm(k_ref[:, cols].astype(F32), kg_ref[...], lo_half).astype(BF16)


def _qk_norm(proj, qg, kg, tm):
    rows = proj.shape[0]
    small = lambda i: (0, 0)
    return pl.pallas_call(
        _qk_norm_kernel,
        grid=(rows // tm,),
        in_specs=[
            pl.BlockSpec((tm, D_NA), lambda i: (i, COL_Q // D_NA)),
            pl.BlockSpec((tm, D_NA), lambda i: (i, COL_K // D_NA)),
            pl.BlockSpec((1, LANES), small),
            pl.BlockSpec((1, LANES), small),
        ],
        out_specs=[pl.BlockSpec((tm, D_NA), lambda i: (i, 0)), pl.BlockSpec((tm, D_NA), lambda i: (i, 0))],
        out_shape=[jax.ShapeDtypeStruct((rows, D_NA), BF16), jax.ShapeDtypeStruct((rows, D_NA), BF16)],
        compiler_params=_cparams(("parallel",)),
        name="qk_norm",
    )(proj, proj, qg, kg)


def _na_kernel(q_ref, kn_ref, vn_ref, km_ref, vm_ref, tb_ref, mb_ref, out_ref,
               k_scr, v_scr, s_scr, p_scr, l_scr, *, grid_rows):
    j = pl.program_id(2)
    lo_half = lax.broadcasted_iota(jnp.int32, (1, LANES), 1) < NA_HEADDIM

    @pl.when(j == 0)
    def _():
        for scr in (k_scr, v_scr):
            scr[0:2 * NA_BLOCK, :] = jnp.zeros((2 * NA_BLOCK, LANES), BF16)

    @pl.when(j > 0)
    def _():
        for scr in (k_scr, v_scr):
            scr[0:NA_BLOCK, :] = scr[NA_BLOCK:2 * NA_BLOCK, :]
            scr[NA_BLOCK:2 * NA_BLOCK, :] = scr[2 * NA_BLOCK:3 * NA_BLOCK, :]

    k_scr[2 * NA_BLOCK:, :] = kn_ref[...]
    v_scr[2 * NA_BLOCK:, :] = vn_ref[...]

    @pl.when(j > 0)
    def _():
        jq = j - 1
        nt = (((1,), (1,)), ((), ()))
        n_keys = WIN_H * GRID_W
        offs = []
        for a in range(NA_ROWS_PER_BLOCK):
            r = NA_ROWS_PER_BLOCK * jq + a
            rstart = jnp.clip(r - WIN_H // 2, 0, grid_rows - WIN_H)
            off = pl.multiple_of((rstart - NA_ROWS_PER_BLOCK * (jq - 1)) * GRID_W, GRID_W)
            offs.append(off)
            q_r = q_ref[a * GRID_W:(a + 1) * GRID_W, :]
            zero = jnp.zeros_like(q_r)
            q2 = jnp.concatenate([jnp.where(lo_half, q_r, zero), jnp.where(lo_half, zero, q_r)], axis=0)
            s_scr[a, :, 0:n_keys] = (
                lax.dot_general(q2, k_scr[pl.ds(off, n_keys), :], nt, preferred_element_type=F32)
                + tb_ref[0, rstart - r + (WIN_H - 1)])
            s_scr[a, :, n_keys:] = lax.dot_general(q2, km_ref[...], nt, preferred_element_type=F32) + mb_ref[0]
        for a in range(NA_ROWS_PER_BLOCK):
            s = s_scr[a]
            p = jnp.exp(s - jnp.max(s, axis=-1, keepdims=True))
            l_scr[a] = jnp.sum(p, axis=-1, keepdims=True)
            p_scr[a] = p.astype(BF16)
        for a in range(NA_ROWS_PER_BLOCK):
            o = (jnp.dot(p_scr[a, :, 0:n_keys], v_scr[pl.ds(offs[a], n_keys), :], preferred_element_type=F32)
                 + jnp.dot(p_scr[a, :, n_keys:], vm_ref[...], preferred_element_type=F32)) / l_scr[a]
            out_ref[a * GRID_W:(a + 1) * GRID_W, :] = jnp.where(lo_half, o[0:GRID_W], o[GRID_W:]).astype(BF16)


def _na_meta_kernel(q_ref, k_ref, v_ref, mb_ref, out_ref):
    lo_half = lax.broadcasted_iota(jnp.int32, (1, LANES), 1) < NA_HEADDIM
    k_meta = k_ref[META_PAD:, :]
    v_meta = v_ref[META_PAD:, :]
    q = q_ref[...]
    nt = (((1,), (1,)), ((), ()))
    outs = []
    for hh in range(2):
        q_h = jnp.where(lo_half == (hh == 0), q, jnp.zeros_like(q))
        s = lax.dot_general(q_h, k_meta, nt, preferred_element_type=F32) + mb_ref[0, hh:hh + 1, 0:N_META]
        p = jnp.exp(s - jnp.max(s, axis=-1, keepdims=True))
        o = jnp.dot(p.astype(BF16), v_meta, preferred_element_type=F32) / jnp.sum(p, axis=-1, keepdims=True)
        outs.append(o)
    out_ref[...] = jnp.where(lo_half, outs[0], outs[1]).astype(BF16)


def _na(qn, kn, proj_tok, kn_meta, proj_meta, tb, mb2, bsz, T):
    nb = T // NA_BLOCK
    vc = COL_V // LANES
    blk = (NA_BLOCK, LANES)
    q_blk = lambda p, b, j: (b * nb + jnp.maximum(j - 1, 0), p)
    kv_blk = lambda col: (lambda p, b, j: (b * nb + jnp.minimum(j, nb - 1), col + p))
    two_heads = 2 * GRID_W
    n_keys = WIN_H * GRID_W + CHUNK
    return pl.pallas_call(
        functools.partial(_na_kernel, grid_rows=T // GRID_W),
        grid=(HEAD_PAIRS_NA, bsz, nb + 1),
        in_specs=[
            pl.BlockSpec(blk, q_blk),
            pl.BlockSpec(blk, kv_blk(0)),
            pl.BlockSpec(blk, kv_blk(vc)),
            pl.BlockSpec((CHUNK, LANES), lambda p, b, j: (0, p)),
            pl.BlockSpec((CHUNK, LANES), lambda p, b, j: (0, vc + p)),
            pl.BlockSpec((1, WIN_H, two_heads, WIN_H * GRID_W), lambda p, b, j: (p, 0, 0, 0)),
            pl.BlockSpec((1, two_heads, LANES), lambda p, b, j: (p, 0, 0)),
        ],
        out_specs=pl.BlockSpec(blk, q_blk),
        out_shape=jax.ShapeDtypeStruct((bsz * T, D_NA), BF16),
        scratch_shapes=[
            pltpu.VMEM((3 * NA_BLOCK, LANES), BF16),
            pltpu.VMEM((3 * NA_BLOCK, LANES), BF16),
            pltpu.VMEM((NA_ROWS_PER_BLOCK, two_heads, n_keys), F32),
            pltpu.VMEM((NA_ROWS_PER_BLOCK, two_heads, n_keys), BF16),
            pltpu.VMEM((NA_ROWS_PER_BLOCK, two_heads, 1), F32),
        ],
        compiler_params=_cparams(("arbitrary", "arbitrary", "arbitrary")),
        name="natten",
    )(qn, kn, proj_tok, kn_meta, proj_meta, tb, mb2)


def _na_meta(qn_meta, kn_meta, proj_meta, mb):
    vc = COL_V // LANES
    blk = (CHUNK, LANES)
    return pl.pallas_call(
        _na_meta_kernel,
        grid=(HEAD_PAIRS_NA,),
        in_specs=[
            pl.BlockSpec(blk, lambda p: (0, p)),
            pl.BlockSpec(blk, lambda p: (0, p)),
            pl.BlockSpec(blk, lambda p: (0, vc + p)),
            pl.BlockSpec((1, 8, LANES), lambda p: (p, 0, 0)),
        ],
        out_specs=pl.BlockSpec(blk, lambda p: (0, p)),
        out_shape=jax.ShapeDtypeStruct((CHUNK, D_NA), BF16),
        compiler_params=_cparams(("arbitrary",)),
        name="natten_meta",
    )(qn_meta, kn_meta, proj_meta, mb)


def _rel_bias_table(rel_bias):
    w = np.arange(GRID_W)
    cstart = np.clip(w - WIN_W // 2, 0, GRID_W - WIN_W)
    cc = np.arange(GRID_W)
    inside = (cc[None, :] >= cstart[:, None]) & (cc[None, :] < cstart[:, None] + WIN_W)
    t = np.arange(2 * WIN_W - 1)
    onehot = ((cc[None, None, :] - w[None, :, None] + (WIN_W - 1)) == t[:, None, None]) & inside[None]
    t15 = jnp.einsum("hrt,twc->hrwc", rel_bias, jnp.asarray(onehot, F32), precision=lax.Precision.HIGHEST)
    t15 = jnp.where(inside[None, None], t15, NEG_BIG)
    per_d0 = [t15[:, d0:d0 + WIN_H].transpose(0, 2, 1, 3).reshape(NA_HEADS, GRID_W, WIN_H * GRID_W)
              for d0 in range(WIN_H)]
    g = jnp.stack(per_d0, axis=1)
    g = g.reshape(HEAD_PAIRS_NA, 2, WIN_H, GRID_W, WIN_H * GRID_W).transpose(0, 2, 1, 3, 4)
    return g.reshape(HEAD_PAIRS_NA, WIN_H, 2 * GRID_W, WIN_H * GRID_W)


def _merge_kernel(ys_ref, yn_ref, gs_ref, gn_ref, h_ref, wbs_ref, wbn_ref, wo_ref, out_ref):
    m_s = jnp.dot(ys_ref[...], wbs_ref[...], preferred_element_type=F32)
    m_n = jnp.dot(yn_ref[...], wbn_ref[...], preferred_element_type=F32)
    merged = _sigmoid(gs_ref[...].astype(F32)) * m_s + _sigmoid(gn_ref[...].astype(F32)) * m_n
    out_ref[...] = h_ref[...] + jnp.dot(merged.astype(BF16), wo_ref[...], preferred_element_type=F32)


def _merge(y_ssd, y_na, proj, h, wbs, wbn, wo, tm, shared_rows):
    rows = y_ssd.shape[0]
    gsc, gnc = COL_GS // D_MODEL, COL_GN // D_MODEL
    r = (lambda i: 0) if shared_rows else (lambda i: i)
    full = lambda i: (0, 0)
    return pl.pallas_call(
        _merge_kernel,
        grid=(rows // tm,),
        in_specs=[
            pl.BlockSpec((tm, D_INNER), lambda i: (i, 0)),
            pl.BlockSpec((tm, D_NA), lambda i: (r(i), 0)),
            pl.BlockSpec((tm, D_MODEL), lambda i: (r(i), gsc)),
            pl.BlockSpec((tm, D_MODEL), lambda i: (r(i), gnc)),
            pl.BlockSpec((tm, D_MODEL), lambda i: (r(i), 0)),
            pl.BlockSpec((D_INNER, D_MODEL), full),
            pl.BlockSpec((D_NA, D_MODEL), full),
            pl.BlockSpec((D_MODEL, D_MODEL), full),
        ],
        out_specs=pl.BlockSpec((tm, D_MODEL), lambda i: (i, 0)),
        out_shape=jax.ShapeDtypeStruct((rows, D_MODEL), F32),
        compiler_params=_cparams(("parallel",)),
        name="merge",
    )(y_ssd, y_na, proj, proj, h, wbs, wbn, wo)


FFN_HALO = 16
FFN_TF = 256


def _ffn_kernel(h_ref, prev_ref, next_ref, meta_ref, g_ref, wup_ref, cw_ref, cb_ref, wd_ref, out_ref,
                u_scr, act_scr, *, tiles_per_seq, tm):
    pos = pl.program_id(0) % tiles_per_seq
    gain = g_ref[...]
    prev = jnp.where(pos == 0, meta_ref[...], prev_ref[...])
    nxt = jnp.where(pos == tiles_per_seq - 1, 0.0, next_ref[...])
    u_scr[0:FFN_HALO, :] = _rmsnorm_rows(prev, gain).astype(BF16)
    u_scr[FFN_HALO:FFN_HALO + tm, :] = _rmsnorm_rows(h_ref[...], gain).astype(BF16)
    u_scr[FFN_HALO + tm:, :] = _rmsnorm_rows(nxt, gain).astype(BF16)
    u = u_scr[...]

    n_ext = tm + 2 * FFN_HALO

    def conv(x_ext, cols):
        y = cw_ref[1:2, cols] * x_ext + cb_ref[:, cols]
        y = y + pltpu.roll(cw_ref[0:1, cols] * x_ext, 1, 0)
        y = y + pltpu.roll(cw_ref[2:3, cols] * x_ext, n_ext - 1, 0)
        return y[FFN_HALO:FFN_HALO + tm]

    for k in range(D_FF // FFN_TF):
        cols_a = slice(k * FFN_TF, (k + 1) * FFN_TF)
        cols_g = slice(D_FF + k * FFN_TF, D_FF + (k + 1) * FFN_TF)
        a = conv(jnp.dot(u, wup_ref[:, cols_a], preferred_element_type=F32), cols_a)
        g = conv(jnp.dot(u, wup_ref[:, cols_g], preferred_element_type=F32), cols_g)
        act_scr[:, cols_a] = (g * _sigmoid(g) * a).astype(BF16)
    out_ref[...] = h_ref[...] + jnp.dot(act_scr[...], wd_ref[...], preferred_element_type=F32)


def _ffn(h1, h1_meta, g_ffn, w_up, cw, cb, w_down, bsz, T, tm):
    rows = bsz * T
    tps = T // tm
    hb = tm // FFN_HALO
    mb = CHUNK // FFN_HALO
    nblk = rows // FFN_HALO
    const = lambda i: (0, 0)
    resident = pl.Buffered(1)
    return pl.pallas_call(
        functools.partial(_ffn_kernel, tiles_per_seq=tps, tm=tm),
        grid=(rows // tm,),
        in_specs=[
            pl.BlockSpec((tm, D_MODEL), lambda i: (i, 0)),
            pl.BlockSpec((FFN_HALO, D_MODEL), lambda i: (jnp.maximum(i * hb - 1, 0), 0)),
            pl.BlockSpec((FFN_HALO, D_MODEL), lambda i: (jnp.minimum((i + 1) * hb, nblk - 1), 0)),
            pl.BlockSpec((FFN_HALO, D_MODEL), lambda i: ((i // tps) * mb + mb - 1, 0)),
            pl.BlockSpec((1, D_MODEL), const),
            pl.BlockSpec((D_MODEL, 2 * D_FF), const, pipeline_mode=resident),
            pl.BlockSpec((8, 2 * D_FF), const),
            pl.BlockSpec((1, 2 * D_FF), const),
            pl.BlockSpec((D_FF, D_MODEL), const, pipeline_mode=resident),
        ],
        out_specs=pl.BlockSpec((tm, D_MODEL), lambda i: (i, 0)),
        out_shape=jax.ShapeDtypeStruct((rows, D_MODEL), F32),
        scratch_shapes=[
            pltpu.VMEM((tm + 2 * FFN_HALO, D_MODEL), BF16),
            pltpu.VMEM((tm, D_FF), BF16),
        ],
        compiler_params=_cparams(("parallel",)),
        name="ffn",
    )(h1, h1, h1, h1_meta, g_ffn, w_up, cw, cb, w_down)


def _pad_rows(a, n):
    return jnp.pad(a, ((0, n - a.shape[0]), (0, 0)))


def _row_tile(rows, cap):
    tm = min(cap, rows)
    assert rows % tm == 0
    return tm


def kernel(x_prompt, x_sample, meta_tokens, g_mix, w_in, ssd_conv_w, ssd_conv_b, dt_bias_f, dt_bias_b,
           a_log_f, a_log_b, d_skip, ssd_norm_g, q_norm_g, k_norm_g, rel_bias, meta_bias, w_br_ssd,
           w_br_na, w_out, g_ffn, w_up, ffn_conv_w, ffn_conv_b, w_down):
    assert g_mix.shape[0] == 1, "single-layer block"
    w = w_in[0]
    o_z, o_xbc = 0, D_INNER
    o_dtf = o_xbc + D_XBC
    o_q = o_dtf + 2 * SSD_HEADS
    w_main = jnp.concatenate([w[:, o_xbc:o_dtf], w[:, o_z:o_xbc], w[:, o_q:]], axis=1).astype(BF16)
    w_dt = jnp.pad(w[:, o_dtf:o_q], ((0, 0), (0, LANES - 2 * SSD_HEADS))).astype(BF16)
    g_mix2 = g_mix.astype(F32)
    lane_pad = lambda v: jnp.pad(v, (0, LANES - v.shape[0]))[None, :].astype(F32)
    dtb = lane_pad(jnp.concatenate([dt_bias_f[0], dt_bias_b[0]]))
    alog = lane_pad(jnp.concatenate([a_log_f[0], a_log_b[0]]))
    dskip = jnp.repeat(d_skip[0].astype(F32), SSD_HEADDIM)[None, :]
    ng = ssd_norm_g.astype(F32)
    conv_w = _pad_rows(ssd_conv_w[0].astype(F32), 8)
    conv_b = ssd_conv_b.astype(F32)
    qg = jnp.tile(q_norm_g[0].astype(F32), 2)[None, :]
    kg = jnp.tile(k_norm_g[0].astype(F32), 2)[None, :]
    tb = _rel_bias_table(rel_bias[0].astype(F32))
    mb = jnp.pad(meta_bias[0].astype(F32).reshape(HEAD_PAIRS_NA, 2, N_META), ((0, 0), (0, 6), (0, LANES - N_META)))
    mb2 = jnp.repeat(meta_bias[0].astype(F32).reshape(HEAD_PAIRS_NA, 2, N_META), GRID_W, axis=1)
    mb2 = jnp.pad(mb2, ((0, 0), (0, 0), (META_PAD, 0)), constant_values=NEG_BIG)
    wbs, wbn, wo = w_br_ssd[0].astype(BF16), w_br_na[0].astype(BF16), w_out[0].astype(BF16)
    w_up_b, w_down_b = w_up[0].astype(BF16), w_down[0].astype(BF16)
    ffn_cw = _pad_rows(ffn_conv_w[0].astype(F32), 8)
    ffn_cb = ffn_conv_b.astype(F32)

    h_meta = jnp.concatenate([jnp.zeros((META_PAD, D_MODEL), F32), meta_tokens.astype(F32)], axis=0)
    proj_meta, dt_meta = _inproj(h_meta, g_mix2, w_main, w_dt, CHUNK)
    dk_meta = _dt_prep(dt_meta, dtb, alog, 1, True)
    qn_meta, kn_meta = _qk_norm(proj_meta, qg, kg, CHUNK)
    y_na_meta = _na_meta(qn_meta, kn_meta, proj_meta, mb)

    def run(x):
        bsz, T, _ = x.shape
        assert T % NA_BLOCK == 0 and T % CHUNK == 0
        x2d = x.reshape(bsz * T, D_MODEL)
        tm = _row_tile(bsz * T, 1024)
        proj_tok, dt_tok = _inproj(x2d, g_mix2, w_main, w_dt, _row_tile(bsz * T, 2048))
        dk_tok = _dt_prep(dt_tok, dtb, alog, DT_PREP_CHUNKS, False)
        y_ssd, y_ssd_meta = _ssd(dk_tok, dk_meta, proj_tok, proj_meta, conv_w, conv_b, dskip, ng, bsz, T)
        qn, kn = _qk_norm(proj_tok, qg, kg, tm)
        y_na = _na(qn, kn, proj_tok, kn_meta, proj_meta, tb, mb2, bsz, T)
        h1 = _merge(y_ssd, y_na, proj_tok, x2d, wbs, wbn, wo, tm, False)
        h1_meta = _merge(y_ssd_meta, y_na_meta, proj_meta, h_meta, wbs, wbn, wo, CHUNK, True)
        tf = _row_tile(T, 1024)
        out = _ffn(h1, h1_meta, g_ffn.astype(F32), w_up_b, ffn_cw, ffn_cb, w_down_b, bsz, T, tf)
        return out.reshape(bsz, T, D_MODEL)

    return (run(x_prompt), run(x_sample))
```

```python
import functools

import numpy as np
import jax
import jax.numpy as jnp
from jax import lax
from jax.experimental import pallas as pl
from jax.experimental.pallas import tpu as pltpu

F32 = jnp.float32
BF16 = jnp.bfloat16

D_MODEL = 1024
N_META = 16
GRID_W = 64
D_INNER = 2048
SSD_HEADDIM = 64
SSD_HEADS = 32
SSD_GROUPS = 8
D_STATE = 128
D_CONV = 5
CHUNK = 128
D_XBC = D_INNER + 2 * SSD_GROUPS * D_STATE
NA_HEADS = 16
NA_HEADDIM = 64
D_NA = NA_HEADS * NA_HEADDIM
WIN_H = 8
WIN_W = 16
D_FF = 2816
FFN_CONV = 3
EPS = 1e-6

LANES = 128
META_PAD = CHUNK - N_META
HEAD_PAIRS_SSD = SSD_HEADS // 2
HEAD_PAIRS_NA = NA_HEADS // 2
NA_ROWS_PER_BLOCK = 16
NA_BLOCK = NA_ROWS_PER_BLOCK * GRID_W
NEG_BIG = -1e30

COL_XBC = 0
COL_Z = COL_XBC + D_XBC
COL_V = COL_Z + D_INNER
COL_GS = COL_V + D_NA
COL_GN = COL_GS + D_MODEL
D_PROJ = COL_GN + D_MODEL
QK_COL_Q = 0
QK_COL_K = D_NA

VMEM_LIMIT = 56 * 1024 * 1024


def _cparams(sem):
    return pltpu.CompilerParams(dimension_semantics=sem, vmem_limit_bytes=VMEM_LIMIT)


def _sigmoid(x):
    return 1.0 / (1.0 + jnp.exp(-x))


def _softplus(x):
    return jnp.maximum(x, 0.0) + jnp.log(1.0 + jnp.exp(-jnp.abs(x)))


def _rmsnorm_rows(x, g):
    ms = jnp.mean(x * x, axis=-1, keepdims=True)
    return x * lax.rsqrt(ms + EPS) * g


INPROJ_TN = 1024


def _inproj_kernel(x_ref, g_ref, w_ref, wdt_ref, out_ref, dt_ref, u_scr):
    @pl.when(pl.program_id(1) == 0)
    def _():
        u = _rmsnorm_rows(x_ref[...], g_ref[...]).astype(BF16)
        u_scr[...] = u
        dt_ref[...] = jnp.dot(u, wdt_ref[...], preferred_element_type=F32)

    out_ref[...] = jnp.dot(u_scr[...], w_ref[...], preferred_element_type=F32).astype(BF16)


def _inproj_qk_kernel(x_ref, g_ref, w_ref, gain_ref, out_ref, u_scr):
    @pl.when(pl.program_id(1) == 0)
    def _():
        u_scr[...] = _rmsnorm_rows(x_ref[...], g_ref[...]).astype(BF16)

    res = jnp.dot(u_scr[...], w_ref[...], preferred_element_type=F32)
    lo_half = lax.broadcasted_iota(jnp.int32, (1, LANES), 1) < NA_HEADDIM
    gain = gain_ref[0]
    for t in range(HEAD_PAIRS_NA):
        cols = slice(t * LANES, (t + 1) * LANES)
        out_ref[:, cols] = _head_rmsnorm(res[:, cols], gain, lo_half).astype(BF16)


def _inproj(x2d, g_mix, w_main, w_dt, w_qk, qk_gain, tm):
    rows = x2d.shape[0]
    tn = INPROJ_TN
    row_blk = pl.BlockSpec((tm, D_MODEL), lambda i, j: (i, 0))
    gain_blk = pl.BlockSpec((1, D_MODEL), lambda i, j: (0, 0))
    proj, dt = pl.pallas_call(
        _inproj_kernel,
        grid=(rows // tm, D_PROJ // tn),
        in_specs=[
            row_blk,
            gain_blk,
            pl.BlockSpec((D_MODEL, tn), lambda i, j: (0, j)),
            pl.BlockSpec((D_MODEL, LANES), lambda i, j: (0, 0)),
        ],
        out_specs=[
            pl.BlockSpec((tm, tn), lambda i, j: (i, j)),
            pl.BlockSpec((tm, LANES), lambda i, j: (i, 0)),
        ],
        out_shape=[
            jax.ShapeDtypeStruct((rows, D_PROJ), BF16),
            jax.ShapeDtypeStruct((rows, LANES), F32),
        ],
        scratch_shapes=[pltpu.VMEM((tm, D_MODEL), BF16)],
        compiler_params=_cparams(("parallel", "arbitrary")),
        name="inproj",
    )(x2d, g_mix, w_main, w_dt)
    qk = pl.pallas_call(
        _inproj_qk_kernel,
        grid=(rows // tm, 2),
        in_specs=[
            row_blk,
            gain_blk,
            pl.BlockSpec((D_MODEL, D_NA), lambda i, j: (0, j)),
            pl.BlockSpec((1, 1, LANES), lambda i, j: (j, 0, 0)),
        ],
        out_specs=pl.BlockSpec((tm, D_NA), lambda i, j: (i, j)),
        out_shape=jax.ShapeDtypeStruct((rows, 2 * D_NA), BF16),
        scratch_shapes=[pltpu.VMEM((tm, D_MODEL), BF16)],
        compiler_params=_cparams(("parallel", "arbitrary")),
        name="inproj_qk",
    )(x2d, g_mix, w_qk, qk_gain)
    return proj, dt, qk


HALO = 16


CONV_PAD = (D_CONV - 1) // 2
CONV_SIDE_TAPS = tuple(k for k in range(D_CONV) if k != CONV_PAD)


def _conv_shift_matrix():
    sh = np.zeros((len(CONV_SIDE_TAPS) * CHUNK, CHUNK + 2 * HALO), np.float32)
    r = np.arange(CHUNK)
    for j, k in enumerate(CONV_SIDE_TAPS):
        sh[j * CHUNK + r, HALO + r + k - CONV_PAD] = 1.0
    return jnp.asarray(sh, BF16)


def _conv_silu_chunk(c, nc, tok_ref, prev_ref, next_ref, meta_ref, shift_ref, w_ref, b_ref, out_ref):
    is_meta = c == 0
    strip = 2 * LANES
    for s in range(D_XBC // strip):
        cols = slice(s * strip, (s + 1) * strip)
        main = jnp.where(is_meta, meta_ref[:, cols], tok_ref[:, cols])
        prev = jnp.where(c == 1, meta_ref[CHUNK - HALO:, cols], prev_ref[:, cols])
        prev = jnp.where(is_meta, jnp.zeros_like(prev), prev)
        nxt = jnp.where(c == nc - 1, jnp.zeros_like(prev), next_ref[:, cols])
        ext = jnp.concatenate([prev, main, nxt], axis=0)
        acc = w_ref[CONV_PAD:CONV_PAD + 1, cols] * main.astype(F32) + b_ref[:, cols]
        for j, k in enumerate(CONV_SIDE_TAPS):
            shifted = jnp.dot(shift_ref[j * CHUNK:(j + 1) * CHUNK, :], ext, preferred_element_type=F32)
            acc = acc + w_ref[k:k + 1, cols] * shifted
        out_ref[:, cols] = (acc * _sigmoid(acc)).astype(BF16)


DK_DT, DK_CS, DK_ECS, DK_CS_T, DK_ECS_T, DK_DT_T = range(6)
DK_N = 6
DT_PREP_CHUNKS = 8


def _dt_prep_kernel(dt_ref, dtb_ref, alog_ref, out_ref, *, n_chunks, zero_pad_rows):
    row = lax.broadcasted_iota(jnp.int32, (CHUNK, LANES), 0)
    col = lax.broadcasted_iota(jnp.int32, (CHUNK, LANES), 1)
    tri = (col <= row).astype(F32)
    neg_a = -jnp.exp(alog_ref[...])
    for i in range(n_chunks):
        dt = _softplus(dt_ref[i * CHUNK:(i + 1) * CHUNK, :] + dtb_ref[...])
        if zero_pad_rows:
            dt = jnp.where(row < META_PAD, 0.0, dt)
        a = dt * neg_a
        cs = jnp.dot(tri, a, preferred_element_type=F32, precision=lax.Precision.HIGHEST)
        ecs = cs - a
        out_ref[i, DK_DT] = dt
        out_ref[i, DK_CS] = cs
        out_ref[i, DK_ECS] = ecs
        out_ref[i, DK_CS_T] = cs.T
        out_ref[i, DK_ECS_T] = ecs.T
        out_ref[i, DK_DT_T] = dt.T


def _dt_prep(dt_raw, dtb, alog, chunks_per_step, zero_pad_rows):
    n = dt_raw.shape[0] // CHUNK
    assert n % chunks_per_step == 0
    small = lambda i: (0, 0)
    return pl.pallas_call(
        functools.partial(_dt_prep_kernel, n_chunks=chunks_per_step, zero_pad_rows=zero_pad_rows),
        grid=(n // chunks_per_step,),
        in_specs=[
            pl.BlockSpec((chunks_per_step * CHUNK, LANES), lambda i: (i, 0)),
            pl.BlockSpec((1, LANES), small),
            pl.BlockSpec((1, LANES), small),
        ],
        out_specs=pl.BlockSpec((chunks_per_step, DK_N, CHUNK, LANES), lambda i: (i, 0, 0, 0)),
        out_shape=jax.ShapeDtypeStruct((n, DK_N, CHUNK, LANES), F32),
        compiler_params=_cparams(("parallel",)),
        name="dt_prep",
    )(dt_raw, dtb, alog)


def _lane_pair(col_lo, col_hi, lo_mask):
    return jnp.where(lo_mask, col_lo, col_hi)


def _decay_tiles(c, dk_tok_ref, dk_meta_ref):
    return lambda k: jnp.where(c == 0, dk_meta_ref[0, k], dk_tok_ref[0, k])


def _ssd_fwd_kernel(tok_ref, prev_ref, next_ref, meta_ref, shift_ref, w_ref, b_ref, dk_tok_ref, dk_meta_ref,
                    xbc_ref, y_ref, state_scr, *, nc):
    c = pl.program_id(1)

    @pl.when(c == 0)
    def _():
        state_scr[...] = jnp.zeros_like(state_scr)

    _conv_silu_chunk(c, nc, tok_ref, prev_ref, next_ref, meta_ref, shift_ref, w_ref, b_ref, xbc_ref)

    dk = _decay_tiles(c, dk_tok_ref, dk_meta_ref)
    dt, cs, ecs, cs_t, ecs_t, dt_t = (dk(k) for k in (DK_DT, DK_CS, DK_ECS, DK_CS_T, DK_ECS_T, DK_DT_T))
    row = lax.broadcasted_iota(jnp.int32, (CHUNK, LANES), 0)
    col = lax.broadcasted_iota(jnp.int32, (CHUNK, LANES), 1)
    lower = row >= col
    strict_lower = row > col
    strict_upper = row < col
    lo_half = col < SSD_HEADDIM
    tot = cs[CHUNK - 1:CHUNK, :]
    heads_per_group = SSD_HEADS // SSD_GROUPS
    gw = heads_per_group * SSD_HEADDIM
    head_of_lane = lax.broadcasted_iota(jnp.int32, (CHUNK, gw), 1) // SSD_HEADDIM

    for g in range(SSD_GROUPS):
        b_g = xbc_ref[:, D_INNER + g * D_STATE:D_INNER + (g + 1) * D_STATE]
        c_g = xbc_ref[:, D_INNER + SSD_GROUPS * D_STATE + g * D_STATE:
                      D_INNER + SSD_GROUPS * D_STATE + (g + 1) * D_STATE]
        cb = lax.dot_general(c_g, b_g, (((1,), (1,)), ((), ())), preferred_element_type=F32)
        st = state_scr[g]
        y_off = jnp.dot(c_g, st.astype(BF16), preferred_element_type=F32)
        x_g = xbc_ref[:, g * gw:(g + 1) * gw]
        m_mats, x_blocks, e_cols, w_cols, decs = [], [], [], [], []
        for r in range(heads_per_group):
            h = g * heads_per_group + r
            hb = SSD_HEADS + h
            csf_col = cs[:, h:h + 1]
            e_mat = jnp.where(lower, csf_col - cs_t[h:h + 1, :], ecs_t[hb:hb + 1, :] - ecs[:, hb:hb + 1])
            dtf_row, dtb_row = dt_t[h:h + 1, :], dt_t[hb:hb + 1, :]
            dsel = jnp.where(strict_lower, dtf_row, jnp.where(strict_upper, dtb_row, dtf_row + dtb_row))
            m_mats.append((cb * jnp.exp(e_mat) * dsel).astype(BF16))
            x_blocks.append(jnp.where(head_of_lane == r, x_g, jnp.zeros_like(x_g)))
            e_cols.append(jnp.exp(csf_col))
            w_cols.append(dt[:, h:h + 1] * jnp.exp(tot[:, h:h + 1] - csf_col))
            decs.append(jnp.exp(tot[:, h:h + 1]))
        y_diag = jnp.dot(jnp.concatenate(m_mats, axis=1), jnp.concatenate(x_blocks, axis=0),
                         preferred_element_type=F32)
        xw_tiles, dec_tiles = [], []
        for pr in range(2):
            tile = g * 2 + pr
            sl = slice(pr * LANES, (pr + 1) * LANES)
            e_pair = _lane_pair(e_cols[2 * pr], e_cols[2 * pr + 1], lo_half)
            y_ref[:, tile * LANES:(tile + 1) * LANES] = y_diag[:, sl] + e_pair * y_off[:, sl]
            w_pair = _lane_pair(w_cols[2 * pr], w_cols[2 * pr + 1], lo_half)
            xw_tiles.append((x_g[:, sl].astype(F32) * w_pair).astype(BF16))
            dec_tiles.append(_lane_pair(decs[2 * pr], decs[2 * pr + 1], lo_half[0:1, :]))
        xw = jnp.concatenate(xw_tiles, axis=1)
        dec = jnp.concatenate(dec_tiles, axis=1)
        upd = lax.dot_general(b_g, xw, (((0,), (0,)), ((), ())), preferred_element_type=F32)
        state_scr[g] = st * dec + upd


def _ssd_bwd_kernel(xbc_ref, dk_tok_ref, dk_meta_ref, z_tok_ref, z_meta_ref, yf_ref,
                    dskip_ref, ng_ref, y_tok_ref, y_meta_ref, state_scr, y_scr, *, nc):
    i = pl.program_id(1)
    c = nc - 1 - i

    @pl.when(i == 0)
    def _():
        state_scr[...] = jnp.zeros_like(state_scr)

    dk = _decay_tiles(c, dk_tok_ref, dk_meta_ref)
    dt, ecs = dk(DK_DT), dk(DK_ECS)
    lo_half = lax.broadcasted_iota(jnp.int32, (CHUNK, LANES), 1) < SSD_HEADDIM
    tot = dk(DK_CS)[CHUNK - 1:CHUNK, :]

    for g in range(SSD_GROUPS):
        b_g = xbc_ref[:, D_INNER + g * D_STATE:D_INNER + (g + 1) * D_STATE]
        c_g = xbc_ref[:, D_INNER + SSD_GROUPS * D_STATE + g * D_STATE:
                      D_INNER + SSD_GROUPS * D_STATE + (g + 1) * D_STATE]
        st = state_scr[g]
        y_off = jnp.dot(c_g, st.astype(BF16), preferred_element_type=F32)
        xw_tiles, dec_tiles = [], []
        for pr in range(2):
            tile = g * 2 + pr
            x_pair = xbc_ref[:, tile * LANES:(tile + 1) * LANES].astype(F32)
            e_cols, w_cols, decs = [], [], []
            for hh in range(2):
                hb = SSD_HEADS + g * 4 + pr * 2 + hh
                ecs_col = ecs[:, hb:hb + 1]
                e_cols.append(jnp.exp(tot[:, hb:hb + 1] - ecs_col))
                w_cols.append(dt[:, hb:hb + 1] * jnp.exp(ecs_col))
                decs.append(jnp.exp(tot[:, hb:hb + 1]))
            sl = slice(tile * LANES, (tile + 1) * LANES)
            y_pair = (yf_ref[:, sl] + _lane_pair(e_cols[0], e_cols[1], lo_half) * y_off[:, pr * LANES:(pr + 1) * LANES]
                      + dskip_ref[:, sl] * x_pair)
            z_pair = jnp.where(c == 0, z_meta_ref[:, sl], z_tok_ref[:, sl]).astype(F32)
            y_scr[:, sl] = y_pair * (z_pair * _sigmoid(z_pair))
            xw_tiles.append((x_pair * _lane_pair(w_cols[0], w_cols[1], lo_half)).astype(BF16))
            dec_tiles.append(_lane_pair(decs[0], decs[1], lo_half[0:1, :]))
        xw = jnp.concatenate(xw_tiles, axis=1)
        dec = jnp.concatenate(dec_tiles, axis=1)
        upd = lax.dot_general(b_g, xw, (((0,), (0,)), ((), ())), preferred_element_type=F32)
        state_scr[g] = st * dec + upd

    gw = D_INNER // SSD_GROUPS
    for g in range(SSD_GROUPS):
        yg = y_scr[:, g * gw:(g + 1) * gw]
        yg = yg * lax.rsqrt(jnp.mean(yg * yg, axis=-1, keepdims=True) + EPS) * ng_ref[:, g * gw:(g + 1) * gw]
        y_scr[:, g * gw:(g + 1) * gw] = yg

    @pl.when(c > 0)
    def _():
        y_tok_ref[...] = y_scr[...].astype(BF16)

    @pl.when(c == 0)
    def _():
        y_meta_ref[...] = y_scr[...].astype(BF16)


def _ssd(dk_tok, dk_meta, proj_tok, proj_meta, conv_w, conv_b, dskip, ng, bsz, T):
    nct = T // CHUNK
    nc = nct + 1
    hb = CHUNK // HALO
    small = lambda b, c: (0, 0)
    dk_blk = (1, DK_N, CHUNK, LANES)
    dk_first = lambda b, c: (0, 0, 0, 0)
    state = pltpu.VMEM((SSD_GROUPS, D_STATE, 4 * SSD_HEADDIM), F32)
    shift = _conv_shift_matrix()
    xbc, y_f = pl.pallas_call(
        functools.partial(_ssd_fwd_kernel, nc=nc),
        grid=(bsz, nc),
        in_specs=[
            pl.BlockSpec((CHUNK, D_XBC), lambda b, c: (b * nct + jnp.maximum(c - 1, 0), 0)),
            pl.BlockSpec((HALO, D_XBC), lambda b, c: (jnp.maximum(b * nct * hb + (c - 1) * hb - 1, 0), 0)),
            pl.BlockSpec((HALO, D_XBC), lambda b, c: (b * nct * hb + jnp.minimum(c, nct - 1) * hb, 0)),
            pl.BlockSpec((CHUNK, D_XBC), small),
            pl.BlockSpec(shift.shape, small),
            pl.BlockSpec((8, D_XBC), small),
            pl.BlockSpec((1, D_XBC), small),
            pl.BlockSpec(dk_blk, lambda b, c: (b * nct + jnp.maximum(c - 1, 0), 0, 0, 0)),
            pl.BlockSpec(dk_blk, dk_first),
        ],
        out_specs=[
            pl.BlockSpec((CHUNK, D_XBC), lambda b, c: (b * nc + c, 0)),
            pl.BlockSpec((CHUNK, D_INNER), lambda b, c: (b * nc + c, 0)),
        ],
        out_shape=[
            jax.ShapeDtypeStruct((bsz * nc * CHUNK, D_XBC), BF16),
            jax.ShapeDtypeStruct((bsz * nc * CHUNK, D_INNER), F32),
        ],
        scratch_shapes=[state],
        compiler_params=_cparams(("parallel", "arbitrary")),
        name="ssd_fwd",
    )(proj_tok, proj_tok, proj_tok, proj_meta, shift, conv_w, conv_b, dk_tok, dk_meta)

    zb = COL_Z // D_INNER
    rev = lambda b, i: nc - 1 - i
    tok_blk = lambda b, i: b * nct + jnp.maximum(rev(b, i) - 1, 0)
    y_tok, y_meta = pl.pallas_call(
        functools.partial(_ssd_bwd_kernel, nc=nc),
        grid=(bsz, nc),
        in_specs=[
            pl.BlockSpec((CHUNK, D_XBC), lambda b, i: (b * nc + rev(b, i), 0)),
            pl.BlockSpec(dk_blk, lambda b, i: (tok_blk(b, i), 0, 0, 0)),
            pl.BlockSpec(dk_blk, dk_first),
            pl.BlockSpec((CHUNK, D_INNER), lambda b, i: (tok_blk(b, i), zb)),
            pl.BlockSpec((CHUNK, D_INNER), lambda b, i: (0, zb)),
            pl.BlockSpec((CHUNK, D_INNER), lambda b, i: (b * nc + rev(b, i), 0)),
            pl.BlockSpec((1, D_INNER), small),
            pl.BlockSpec((1, D_INNER), small),
        ],
        out_specs=[
            pl.BlockSpec((CHUNK, D_INNER), lambda b, i: (tok_blk(b, i), 0)),
            pl.BlockSpec((CHUNK, D_INNER), lambda b, i: (b, 0)),
        ],
        out_shape=[
            jax.ShapeDtypeStruct((bsz * T, D_INNER), BF16),
            jax.ShapeDtypeStruct((bsz * CHUNK, D_INNER), BF16),
        ],
        scratch_shapes=[state, pltpu.VMEM((CHUNK, D_INNER), F32)],
        compiler_params=_cparams(("parallel", "arbitrary")),
        name="ssd_bwd",
    )(xbc, dk_tok, dk_meta, proj_tok, proj_meta, y_f, dskip, ng)
    return y_tok, y_meta


def _head_rmsnorm(x, gain, lo_half):
    sq = x * x
    s_lo = jnp.sum(jnp.where(lo_half, sq, 0.0), axis=-1, keepdims=True)
    s_all = jnp.sum(sq, axis=-1, keepdims=True)
    r_lo = lax.rsqrt(s_lo * (1.0 / NA_HEADDIM) + EPS)
    r_hi = lax.rsqrt((s_all - s_lo) * (1.0 / NA_HEADDIM) + EPS)
    return x * jnp.where(lo_half, r_lo, r_hi) * gain


def _na_kernel(q_ref, kn_ref, vn_ref, km_ref, vm_ref, tb_ref, mb_ref, out_ref,
               k_scr, v_scr, s_scr, p_scr, l_scr, *, grid_rows):
    j = pl.program_id(2)
    lo_half = lax.broadcasted_iota(jnp.int32, (1, LANES), 1) < NA_HEADDIM

    @pl.when(j == 0)
    def _():
        for scr in (k_scr, v_scr):
            scr[0:2 * NA_BLOCK, :] = jnp.zeros((2 * NA_BLOCK, LANES), BF16)

    @pl.when(j > 0)
    def _():
        for scr in (k_scr, v_scr):
            scr[0:NA_BLOCK, :] = scr[NA_BLOCK:2 * NA_BLOCK, :]
            scr[NA_BLOCK:2 * NA_BLOCK, :] = scr[2 * NA_BLOCK:3 * NA_BLOCK, :]

    k_scr[2 * NA_BLOCK:, :] = kn_ref[...]
    v_scr[2 * NA_BLOCK:, :] = vn_ref[...]

    @pl.when(j > 0)
    def _():
        jq = j - 1
        nt = (((1,), (1,)), ((), ()))
        n_keys = WIN_H * GRID_W
        offs = []
        for a in range(NA_ROWS_PER_BLOCK):
            r = NA_ROWS_PER_BLOCK * jq + a
            rstart = jnp.clip(r - WIN_H // 2, 0, grid_rows - WIN_H)
            off = pl.multiple_of((rstart - NA_ROWS_PER_BLOCK * (jq - 1)) * GRID_W, GRID_W)
            offs.append(off)
            q_r = q_ref[a * GRID_W:(a + 1) * GRID_W, :]
            zero = jnp.zeros_like(q_r)
            q2 = jnp.concatenate([jnp.where(lo_half, q_r, zero), jnp.where(lo_half, zero, q_r)], axis=0)
            s_scr[a, :, 0:n_keys] = (
                lax.dot_general(q2, k_scr[pl.ds(off, n_keys), :], nt, preferred_element_type=F32)
                + tb_ref[0, rstart - r + (WIN_H - 1)])
            s_scr[a, :, n_keys:] = lax.dot_general(q2, km_ref[...], nt, preferred_element_type=F32) + mb_ref[0]
        for a in range(NA_ROWS_PER_BLOCK):
            s = s_scr[a]
            p = jnp.exp(s - jnp.max(s, axis=-1, keepdims=True))
            l_scr[a] = jnp.sum(p, axis=-1, keepdims=True)
            p_scr[a] = p.astype(BF16)
        for a in range(NA_ROWS_PER_BLOCK):
            o = (jnp.dot(p_scr[a, :, 0:n_keys], v_scr[pl.ds(offs[a], n_keys), :], preferred_element_type=F32)
                 + jnp.dot(p_scr[a, :, n_keys:], vm_ref[...], preferred_element_type=F32)) / l_scr[a]
            out_ref[a * GRID_W:(a + 1) * GRID_W, :] = jnp.where(lo_half, o[0:GRID_W], o[GRID_W:]).astype(BF16)


def _na_meta_kernel(q_ref, k_ref, v_ref, mb_ref, out_ref):
    lo_half = lax.broadcasted_iota(jnp.int32, (1, LANES), 1) < NA_HEADDIM
    k_meta = k_ref[META_PAD:, :]
    v_meta = v_ref[META_PAD:, :]
    q = q_ref[...]
    nt = (((1,), (1,)), ((), ()))
    outs = []
    for hh in range(2):
        q_h = jnp.where(lo_half == (hh == 0), q, jnp.zeros_like(q))
        s = lax.dot_general(q_h, k_meta, nt, preferred_element_type=F32) + mb_ref[0, hh:hh + 1, 0:N_META]
        p = jnp.exp(s - jnp.max(s, axis=-1, keepdims=True))
        o = jnp.dot(p.astype(BF16), v_meta, preferred_element_type=F32) / jnp.sum(p, axis=-1, keepdims=True)
        outs.append(o)
    out_ref[...] = jnp.where(lo_half, outs[0], outs[1]).astype(BF16)


def _na(qk_tok, proj_tok, qk_meta, proj_meta, tb, mb2, bsz, T):
    nb = T // NA_BLOCK
    qc, kc, vc = QK_COL_Q // LANES, QK_COL_K // LANES, COL_V // LANES
    blk = (NA_BLOCK, LANES)
    q_blk = lambda p, b, j: (b * nb + jnp.maximum(j - 1, 0), qc + p)
    kv_blk = lambda col: (lambda p, b, j: (b * nb + jnp.minimum(j, nb - 1), col + p))
    two_heads = 2 * GRID_W
    n_keys = WIN_H * GRID_W + CHUNK
    return pl.pallas_call(
        functools.partial(_na_kernel, grid_rows=T // GRID_W),
        grid=(HEAD_PAIRS_NA, bsz, nb + 1),
        in_specs=[
            pl.BlockSpec(blk, q_blk),
            pl.BlockSpec(blk, kv_blk(kc)),
            pl.BlockSpec(blk, kv_blk(vc)),
            pl.BlockSpec((CHUNK, LANES), lambda p, b, j: (0, kc + p)),
            pl.BlockSpec((CHUNK, LANES), lambda p, b, j: (0, vc + p)),
            pl.BlockSpec((1, WIN_H, two_heads, WIN_H * GRID_W), lambda p, b, j: (p, 0, 0, 0)),
            pl.BlockSpec((1, two_heads, LANES), lambda p, b, j: (p, 0, 0)),
        ],
        out_specs=pl.BlockSpec(blk, lambda p, b, j: (b * nb + jnp.maximum(j - 1, 0), p)),
        out_shape=jax.ShapeDtypeStruct((bsz * T, D_NA), BF16),
        scratch_shapes=[
            pltpu.VMEM((3 * NA_BLOCK, LANES), BF16),
            pltpu.VMEM((3 * NA_BLOCK, LANES), BF16),
            pltpu.VMEM((NA_ROWS_PER_BLOCK, two_heads, n_keys), F32),
            pltpu.VMEM((NA_ROWS_PER_BLOCK, two_heads, n_keys), BF16),
            pltpu.VMEM((NA_ROWS_PER_BLOCK, two_heads, 1), F32),
        ],
        compiler_params=_cparams(("arbitrary", "arbitrary", "arbitrary")),
        name="natten",
    )(qk_tok, qk_tok, proj_tok, qk_meta, proj_meta, tb, mb2)


def _na_meta(qk_meta, proj_meta, mb):
    qc, kc, vc = QK_COL_Q // LANES, QK_COL_K // LANES, COL_V // LANES
    blk = (CHUNK, LANES)
    return pl.pallas_call(
        _na_meta_kernel,
        grid=(HEAD_PAIRS_NA,),
        in_specs=[
            pl.BlockSpec(blk, lambda p: (0, qc + p)),
            pl.BlockSpec(blk, lambda p: (0, kc + p)),
            pl.BlockSpec(blk, lambda p: (0, vc + p)),
            pl.BlockSpec((1, 8, LANES), lambda p: (p, 0, 0)),
        ],
        out_specs=pl.BlockSpec(blk, lambda p: (0, p)),
        out_shape=jax.ShapeDtypeStruct((CHUNK, D_NA), BF16),
        compiler_params=_cparams(("arbitrary",)),
        name="natten_meta",
    )(qk_meta, qk_meta, proj_meta, mb)


def _rel_bias_table(rel_bias):
    w = np.arange(GRID_W)
    cstart = np.clip(w - WIN_W // 2, 0, GRID_W - WIN_W)
    cc = np.arange(GRID_W)
    inside = (cc[None, :] >= cstart[:, None]) & (cc[None, :] < cstart[:, None] + WIN_W)
    t = np.arange(2 * WIN_W - 1)
    onehot = ((cc[None, None, :] - w[None, :, None] + (WIN_W - 1)) == t[:, None, None]) & inside[None]
    rows = jnp.stack([rel_bias[:, d0:d0 + WIN_H] for d0 in range(WIN_H)], axis=1)
    rows = rows.reshape(HEAD_PAIRS_NA, 2, WIN_H, WIN_H, 2 * WIN_W - 1)
    g = jnp.einsum("phdit,twc->pdhwic", rows, jnp.asarray(onehot, F32), precision=lax.Precision.HIGHEST)
    g = jnp.where(inside[None, None, None, :, None, :], g, NEG_BIG)
    return g.reshape(HEAD_PAIRS_NA, WIN_H, 2 * GRID_W, WIN_H * GRID_W)


def _merge_kernel(ys_ref, yn_ref, gs_ref, gn_ref, h_ref, wbs_ref, wbn_ref, wo_ref, out_ref):
    m_s = jnp.dot(ys_ref[...], wbs_ref[...], preferred_element_type=F32)
    m_n = jnp.dot(yn_ref[...], wbn_ref[...], preferred_element_type=F32)
    merged = _sigmoid(gs_ref[...].astype(F32)) * m_s + _sigmoid(gn_ref[...].astype(F32)) * m_n
    out_ref[...] = h_ref[...] + jnp.dot(merged.astype(BF16), wo_ref[...], preferred_element_type=F32)


def _merge(y_ssd, y_na, proj, h, wbs, wbn, wo, tm, shared_rows):
    rows = y_ssd.shape[0]
    gsc, gnc = COL_GS // D_MODEL, COL_GN // D_MODEL
    r = (lambda i: 0) if shared_rows else (lambda i: i)
    full = lambda i: (0, 0)
    return pl.pallas_call(
        _merge_kernel,
        grid=(rows // tm,),
        in_specs=[
            pl.BlockSpec((tm, D_INNER), lambda i: (i, 0)),
            pl.BlockSpec((tm, D_NA), lambda i: (r(i), 0)),
            pl.BlockSpec((tm, D_MODEL), lambda i: (r(i), gsc)),
            pl.BlockSpec((tm, D_MODEL), lambda i: (r(i), gnc)),
            pl.BlockSpec((tm, D_MODEL), lambda i: (r(i), 0)),
            pl.BlockSpec((D_INNER, D_MODEL), full),
            pl.BlockSpec((D_NA, D_MODEL), full),
            pl.BlockSpec((D_MODEL, D_MODEL), full),
        ],
        out_specs=pl.BlockSpec((tm, D_MODEL), lambda i: (i, 0)),
        out_shape=jax.ShapeDtypeStruct((rows, D_MODEL), F32),
        compiler_params=_cparams(("parallel",)),
        name="merge",
    )(y_ssd, y_na, proj, proj, h, wbs, wbn, wo)


FFN_HALO = 16
FFN_TF = 256


def _ffn_kernel(h_ref, prev_ref, next_ref, meta_ref, g_ref, wup_ref, cw_ref, cb_ref, wd_ref, out_ref,
                u_scr, act_scr, *, tiles_per_seq, tm):
    pos = pl.program_id(0) % tiles_per_seq
    gain = g_ref[...]
    prev = jnp.where(pos == 0, meta_ref[...], prev_ref[...])
    nxt = jnp.where(pos == tiles_per_seq - 1, 0.0, next_ref[...])
    u_scr[0:FFN_HALO, :] = _rmsnorm_rows(prev, gain).astype(BF16)
    u_scr[FFN_HALO:FFN_HALO + tm, :] = _rmsnorm_rows(h_ref[...], gain).astype(BF16)
    u_scr[FFN_HALO + tm:, :] = _rmsnorm_rows(nxt, gain).astype(BF16)
    u = u_scr[...]

    n_ext = tm + 2 * FFN_HALO

    def conv(x_ext, cols):
        y = cw_ref[1:2, cols] * x_ext + cb_ref[:, cols]
        y = y + pltpu.roll(cw_ref[0:1, cols] * x_ext, 1, 0)
        y = y + pltpu.roll(cw_ref[2:3, cols] * x_ext, n_ext - 1, 0)
        return y[FFN_HALO:FFN_HALO + tm]

    for k in range(D_FF // FFN_TF):
        cols_a = slice(k * FFN_TF, (k + 1) * FFN_TF)
        cols_g = slice(D_FF + k * FFN_TF, D_FF + (k + 1) * FFN_TF)
        a = conv(jnp.dot(u, wup_ref[:, cols_a], preferred_element_type=F32), cols_a)
        g = conv(jnp.dot(u, wup_ref[:, cols_g], preferred_element_type=F32), cols_g)
        act_scr[:, cols_a] = (g * _sigmoid(g) * a).astype(BF16)
    out_ref[...] = h_ref[...] + jnp.dot(act_scr[...], wd_ref[...], preferred_element_type=F32)


def _ffn(h1, h1_meta, g_ffn, w_up, cw, cb, w_down, bsz, T, tm):
    rows = bsz * T
    tps = T // tm
    hb = tm // FFN_HALO
    mb = CHUNK // FFN_HALO
    nblk = rows // FFN_HALO
    const = lambda i: (0, 0)
    resident = pl.Buffered(1)
    return pl.pallas_call(
        functools.partial(_ffn_kernel, tiles_per_seq=tps, tm=tm),
        grid=(rows // tm,),
        in_specs=[
            pl.BlockSpec((tm, D_MODEL), lambda i: (i, 0)),
            pl.BlockSpec((FFN_HALO, D_MODEL), lambda i: (jnp.maximum(i * hb - 1, 0), 0)),
            pl.BlockSpec((FFN_HALO, D_MODEL), lambda i: (jnp.minimum((i + 1) * hb, nblk - 1), 0)),
            pl.BlockSpec((FFN_HALO, D_MODEL), lambda i: ((i // tps) * mb + mb - 1, 0)),
            pl.BlockSpec((1, D_MODEL), const),
            pl.BlockSpec((D_MODEL, 2 * D_FF), const, pipeline_mode=resident),
            pl.BlockSpec((8, 2 * D_FF), const),
            pl.BlockSpec((1, 2 * D_FF), const),
            pl.BlockSpec((D_FF, D_MODEL), const, pipeline_mode=resident),
        ],
        out_specs=pl.BlockSpec((tm, D_MODEL), lambda i: (i, 0)),
        out_shape=jax.ShapeDtypeStruct((rows, D_MODEL), F32),
        scratch_shapes=[
            pltpu.VMEM((tm + 2 * FFN_HALO, D_MODEL), BF16),
            pltpu.VMEM((tm, D_FF), BF16),
        ],
        compiler_params=_cparams(("parallel",)),
        name="ffn",
    )(h1, h1, h1, h1_meta, g_ffn, w_up, cw, cb, w_down)


def _pad_rows(a, n):
    return jnp.pad(a, ((0, n - a.shape[0]), (0, 0)))


def _row_tile(rows, cap):
    tm = min(cap, rows)
    assert rows % tm == 0
    return tm


def kernel(x_prompt, x_sample, meta_tokens, g_mix, w_in, ssd_conv_w, ssd_conv_b, dt_bias_f, dt_bias_b,
           a_log_f, a_log_b, d_skip, ssd_norm_g, q_norm_g, k_norm_g, rel_bias, meta_bias, w_br_ssd,
           w_br_na, w_out, g_ffn, w_up, ffn_conv_w, ffn_conv_b, w_down):
    assert g_mix.shape[0] == 1, "single-layer block"
    w = w_in[0]
    o_z, o_xbc = 0, D_INNER
    o_dtf = o_xbc + D_XBC
    o_q = o_dtf + 2 * SSD_HEADS
    o_v = o_q + 2 * D_NA
    w_main = jnp.concatenate([w[:, o_xbc:o_dtf], w[:, o_z:o_xbc], w[:, o_v:]], axis=1).astype(BF16)
    w_qk = w[:, o_q:o_v].astype(BF16)
    w_dt = jnp.pad(w[:, o_dtf:o_q], ((0, 0), (0, LANES - 2 * SSD_HEADS))).astype(BF16)
    g_mix2 = g_mix.astype(F32)
    lane_pad = lambda v: jnp.pad(v, (0, LANES - v.shape[0]))[None, :].astype(F32)
    dtb = lane_pad(jnp.concatenate([dt_bias_f[0], dt_bias_b[0]]))
    alog = lane_pad(jnp.concatenate([a_log_f[0], a_log_b[0]]))
    dskip = jnp.repeat(d_skip[0].astype(F32), SSD_HEADDIM)[None, :]
    ng = ssd_norm_g.astype(F32)
    conv_w = _pad_rows(ssd_conv_w[0].astype(F32), 8)
    conv_b = ssd_conv_b.astype(F32)
    qg = jnp.tile(q_norm_g[0].astype(F32), 2)[None, :] * (NA_HEADDIM ** -0.5)
    kg = jnp.tile(k_norm_g[0].astype(F32), 2)[None, :]
    qk_gain = jnp.stack([qg, kg])
    tb = _rel_bias_table(rel_bias[0].astype(F32))
    mb = jnp.pad(meta_bias[0].astype(F32).reshape(HEAD_PAIRS_NA, 2, N_META), ((0, 0), (0, 6), (0, LANES - N_META)))
    mb2 = jnp.repeat(meta_bias[0].astype(F32).reshape(HEAD_PAIRS_NA, 2, N_META), GRID_W, axis=1)
    mb2 = jnp.pad(mb2, ((0, 0), (0, 0), (META_PAD, 0)), constant_values=NEG_BIG)
    wbs, wbn, wo = w_br_ssd[0].astype(BF16), w_br_na[0].astype(BF16), w_out[0].astype(BF16)
    w_up_b, w_down_b = w_up[0].astype(BF16), w_down[0].astype(BF16)
    ffn_cw = _pad_rows(ffn_conv_w[0].astype(F32), 8)
    ffn_cb = ffn_conv_b.astype(F32)

    h_meta = jnp.concatenate([jnp.zeros((META_PAD, D_MODEL), F32), meta_tokens.astype(F32)], axis=0)
    proj_meta, dt_meta, qk_meta = _inproj(h_meta, g_mix2, w_main, w_dt, w_qk, qk_gain, CHUNK)
    dk_meta = _dt_prep(dt_meta, dtb, alog, 1, True)
    y_na_meta = _na_meta(qk_meta, proj_meta, mb)

    def run(x):
        bsz, T, _ = x.shape
        assert T % NA_BLOCK == 0 and T % CHUNK == 0
        x2d = x.reshape(bsz * T, D_MODEL)
        tm = _row_tile(bsz * T, 1024)
        proj_tok, dt_tok, qk_tok = _inproj(x2d, g_mix2, w_main, w_dt, w_qk, qk_gain, _row_tile(bsz * T, 2048))
        dk_tok = _dt_prep(dt_tok, dtb, alog, DT_PREP_CHUNKS, False)
        y_ssd, y_ssd_meta = _ssd(dk_tok, dk_meta, proj_tok, proj_meta, conv_w, conv_b, dskip, ng, bsz, T)
        y_na = _na(qk_tok, proj_tok, qk_meta, proj_meta, tb, mb2, bsz, T)
        h1 = _merge(y_ssd, y_na, proj_tok, x2d, wbs, wbn, wo, tm, False)
        h1_meta = _merge(y_ssd_meta, y_na_meta, proj_meta, h_meta, wbs, wbn, wo, CHUNK, True)
        tf = _row_tile(T, 1024)
        out = _ffn(h1, h1_meta, g_ffn.astype(F32), w_up_b, ffn_cw, ffn_cb, w_down_b, bsz, T, tf)
        return out.reshape(bsz, T, D_MODEL)

    return (run(x_prompt), run(x_sample))
```

```python
import functools

import numpy as np
import jax
import jax.numpy as jnp
from jax import lax
from jax.experimental import pallas as pl
from jax.experimental.pallas import tpu as pltpu

F32 = jnp.float32
BF16 = jnp.bfloat16

D_MODEL = 1024
N_META = 16
GRID_W = 64
D_INNER = 2048
SSD_HEADDIM = 64
SSD_HEADS = 32
SSD_GROUPS = 8
D_STATE = 128
D_CONV = 5
CHUNK = 128
D_XBC = D_INNER + 2 * SSD_GROUPS * D_STATE
NA_HEADS = 16
NA_HEADDIM = 64
D_NA = NA_HEADS * NA_HEADDIM
WIN_H = 8
WIN_W = 16
D_FF = 2816
FFN_CONV = 3
EPS = 1e-6

LANES = 128
META_PAD = CHUNK - N_META
HEAD_PAIRS_SSD = SSD_HEADS // 2
HEAD_PAIRS_NA = NA_HEADS // 2
NA_ROWS_PER_BLOCK = 16
NA_BLOCK = NA_ROWS_PER_BLOCK * GRID_W
NA_PAIRS_PER_STEP = 2
NEG_BIG = -1e30

COL_XBC = 0
COL_Z = COL_XBC + D_XBC
COL_V = COL_Z + D_INNER
COL_GS = COL_V + D_NA
COL_GN = COL_GS + D_MODEL
D_PROJ = COL_GN + D_MODEL
QK_COL_Q = 0
QK_COL_K = D_NA

VMEM_LIMIT = 56 * 1024 * 1024


def _cparams(sem):
    return pltpu.CompilerParams(dimension_semantics=sem, vmem_limit_bytes=VMEM_LIMIT)


def _sigmoid(x):
    return 1.0 / (1.0 + jnp.exp(-x))


def _softplus(x):
    return jnp.maximum(x, 0.0) + jnp.log(1.0 + jnp.exp(-jnp.abs(x)))


def _rmsnorm_rows(x, g):
    ms = jnp.mean(x * x, axis=-1, keepdims=True)
    return x * lax.rsqrt(ms + EPS) * g


INPROJ_TN = 1024


def _inproj_kernel(x_ref, g_ref, w_ref, wdt_ref, out_ref, dt_ref, u_scr):
    @pl.when(pl.program_id(1) == 0)
    def _():
        u = _rmsnorm_rows(x_ref[...], g_ref[...]).astype(BF16)
        u_scr[...] = u
        dt_ref[...] = jnp.dot(u, wdt_ref[...], preferred_element_type=F32)

    out_ref[...] = jnp.dot(u_scr[...], w_ref[...], preferred_element_type=F32).astype(BF16)


def _inproj_qk_kernel(x_ref, g_ref, w_ref, gain_ref, out_ref, u_scr):
    @pl.when(pl.program_id(1) == 0)
    def _():
        u_scr[...] = _rmsnorm_rows(x_ref[...], g_ref[...]).astype(BF16)

    res = jnp.dot(u_scr[...], w_ref[...], preferred_element_type=F32)
    lo_half = lax.broadcasted_iota(jnp.int32, (1, LANES), 1) < NA_HEADDIM
    gain = gain_ref[0]
    for t in range(HEAD_PAIRS_NA):
        cols = slice(t * LANES, (t + 1) * LANES)
        out_ref[:, cols] = _head_rmsnorm(res[:, cols], gain, lo_half).astype(BF16)


def _inproj(x2d, g_mix, w_main, w_dt, w_qk, qk_gain, tm):
    rows = x2d.shape[0]
    tn = INPROJ_TN
    row_blk = pl.BlockSpec((tm, D_MODEL), lambda i, j: (i, 0))
    gain_blk = pl.BlockSpec((1, D_MODEL), lambda i, j: (0, 0))
    proj, dt = pl.pallas_call(
        _inproj_kernel,
        grid=(rows // tm, D_PROJ // tn),
        in_specs=[
            row_blk,
            gain_blk,
            pl.BlockSpec((D_MODEL, tn), lambda i, j: (0, j)),
            pl.BlockSpec((D_MODEL, LANES), lambda i, j: (0, 0)),
        ],
        out_specs=[
            pl.BlockSpec((tm, tn), lambda i, j: (i, j)),
            pl.BlockSpec((tm, LANES), lambda i, j: (i, 0)),
        ],
        out_shape=[
            jax.ShapeDtypeStruct((rows, D_PROJ), BF16),
            jax.ShapeDtypeStruct((rows, LANES), F32),
        ],
        scratch_shapes=[pltpu.VMEM((tm, D_MODEL), BF16)],
        compiler_params=_cparams(("parallel", "arbitrary")),
        name="inproj",
    )(x2d, g_mix, w_main, w_dt)
    qk = pl.pallas_call(
        _inproj_qk_kernel,
        grid=(rows // tm, 2),
        in_specs=[
            row_blk,
            gain_blk,
            pl.BlockSpec((D_MODEL, D_NA), lambda i, j: (0, j)),
            pl.BlockSpec((1, 1, LANES), lambda i, j: (j, 0, 0)),
        ],
        out_specs=pl.BlockSpec((tm, D_NA), lambda i, j: (i, j)),
        out_shape=jax.ShapeDtypeStruct((rows, 2 * D_NA), BF16),
        scratch_shapes=[pltpu.VMEM((tm, D_MODEL), BF16)],
        compiler_params=_cparams(("parallel", "arbitrary")),
        name="inproj_qk",
    )(x2d, g_mix, w_qk, qk_gain)
    return proj, dt, qk


HALO = 16


CONV_PAD = (D_CONV - 1) // 2
CONV_SIDE_TAPS = tuple(k for k in range(D_CONV) if k != CONV_PAD)


def _conv_shift_matrix():
    sh = np.zeros((len(CONV_SIDE_TAPS) * CHUNK, CHUNK + 2 * HALO), np.float32)
    r = np.arange(CHUNK)
    for j, k in enumerate(CONV_SIDE_TAPS):
        sh[j * CHUNK + r, HALO + r + k - CONV_PAD] = 1.0
    return jnp.asarray(sh, BF16)


def _conv_silu_chunk(c, nc, tok_ref, prev_ref, next_ref, meta_ref, shift_ref, w_ref, b_ref, out_ref):
    is_meta = c == 0
    strip = 2 * LANES
    for s in range(D_XBC // strip):
        cols = slice(s * strip, (s + 1) * strip)
        main = jnp.where(is_meta, meta_ref[:, cols], tok_ref[:, cols])
        prev = jnp.where(c == 1, meta_ref[CHUNK - HALO:, cols], prev_ref[:, cols])
        prev = jnp.where(is_meta, jnp.zeros_like(prev), prev)
        nxt = jnp.where(c == nc - 1, jnp.zeros_like(prev), next_ref[:, cols])
        ext = jnp.concatenate([prev, main, nxt], axis=0)
        acc = w_ref[CONV_PAD:CONV_PAD + 1, cols] * main.astype(F32) + b_ref[:, cols]
        for j, k in enumerate(CONV_SIDE_TAPS):
            shifted = jnp.dot(shift_ref[j * CHUNK:(j + 1) * CHUNK, :], ext, preferred_element_type=F32)
            acc = acc + w_ref[k:k + 1, cols] * shifted
        out_ref[:, cols] = (acc * _sigmoid(acc)).astype(BF16)


DK_DT, DK_CS, DK_ECS, DK_CS_T, DK_ECS_T, DK_DT_T = range(6)
DK_N = 6
DT_PREP_CHUNKS = 8


def _dt_prep_kernel(dt_ref, dtb_ref, alog_ref, out_ref, *, n_chunks, zero_pad_rows):
    row = lax.broadcasted_iota(jnp.int32, (CHUNK, LANES), 0)
    col = lax.broadcasted_iota(jnp.int32, (CHUNK, LANES), 1)
    tri = (col <= row).astype(F32)
    neg_a = -jnp.exp(alog_ref[...])
    for i in range(n_chunks):
        dt = _softplus(dt_ref[i * CHUNK:(i + 1) * CHUNK, :] + dtb_ref[...])
        if zero_pad_rows:
            dt = jnp.where(row < META_PAD, 0.0, dt)
        a = dt * neg_a
        cs = jnp.dot(tri, a, preferred_element_type=F32, precision=lax.Precision.HIGHEST)
        ecs = cs - a
        out_ref[i, DK_DT] = dt
        out_ref[i, DK_CS] = cs
        out_ref[i, DK_ECS] = ecs
        out_ref[i, DK_CS_T] = cs.T
        out_ref[i, DK_ECS_T] = ecs.T
        out_ref[i, DK_DT_T] = dt.T


def _dt_prep(dt_raw, dtb, alog, chunks_per_step, zero_pad_rows):
    n = dt_raw.shape[0] // CHUNK
    assert n % chunks_per_step == 0
    small = lambda i: (0, 0)
    return pl.pallas_call(
        functools.partial(_dt_prep_kernel, n_chunks=chunks_per_step, zero_pad_rows=zero_pad_rows),
        grid=(n // chunks_per_step,),
        in_specs=[
            pl.BlockSpec((chunks_per_step * CHUNK, LANES), lambda i: (i, 0)),
            pl.BlockSpec((1, LANES), small),
            pl.BlockSpec((1, LANES), small),
        ],
        out_specs=pl.BlockSpec((chunks_per_step, DK_N, CHUNK, LANES), lambda i: (i, 0, 0, 0)),
        out_shape=jax.ShapeDtypeStruct((n, DK_N, CHUNK, LANES), F32),
        compiler_params=_cparams(("parallel",)),
        name="dt_prep",
    )(dt_raw, dtb, alog)


def _lane_pair(col_lo, col_hi, lo_mask):
    return jnp.where(lo_mask, col_lo, col_hi)


def _decay_tiles(c, dk_tok_ref, dk_meta_ref):
    return lambda k: jnp.where(c == 0, dk_meta_ref[0, k], dk_tok_ref[0, k])


def _ssd_fwd_kernel(tok_ref, prev_ref, next_ref, meta_ref, shift_ref, w_ref, b_ref, dk_tok_ref, dk_meta_ref,
                    xbc_ref, y_ref, state_scr, *, nc):
    c = pl.program_id(1)

    @pl.when(c == 0)
    def _():
        state_scr[...] = jnp.zeros_like(state_scr)

    _conv_silu_chunk(c, nc, tok_ref, prev_ref, next_ref, meta_ref, shift_ref, w_ref, b_ref, xbc_ref)

    dk = _decay_tiles(c, dk_tok_ref, dk_meta_ref)
    dt, cs, ecs, cs_t, ecs_t, dt_t = (dk(k) for k in (DK_DT, DK_CS, DK_ECS, DK_CS_T, DK_ECS_T, DK_DT_T))
    row = lax.broadcasted_iota(jnp.int32, (CHUNK, LANES), 0)
    col = lax.broadcasted_iota(jnp.int32, (CHUNK, LANES), 1)
    lower = row >= col
    strict_lower = row > col
    strict_upper = row < col
    lo_half = col < SSD_HEADDIM
    tot = cs[CHUNK - 1:CHUNK, :]
    heads_per_group = SSD_HEADS // SSD_GROUPS
    gw = heads_per_group * SSD_HEADDIM
    head_of_lane = lax.broadcasted_iota(jnp.int32, (CHUNK, gw), 1) // SSD_HEADDIM

    for g in range(SSD_GROUPS):
        b_g = xbc_ref[:, D_INNER + g * D_STATE:D_INNER + (g + 1) * D_STATE]
        c_g = xbc_ref[:, D_INNER + SSD_GROUPS * D_STATE + g * D_STATE:
                      D_INNER + SSD_GROUPS * D_STATE + (g + 1) * D_STATE]
        cb = lax.dot_general(c_g, b_g, (((1,), (1,)), ((), ())), preferred_element_type=F32)
        st = state_scr[g]
        y_off = jnp.dot(c_g, st.astype(BF16), preferred_element_type=F32)
        x_g = xbc_ref[:, g * gw:(g + 1) * gw]
        m_mats, x_blocks, e_cols, w_cols, decs = [], [], [], [], []
        for r in range(heads_per_group):
            h = g * heads_per_group + r
            hb = SSD_HEADS + h
            csf_col = cs[:, h:h + 1]
            e_mat = jnp.where(lower, csf_col - cs_t[h:h + 1, :], ecs_t[hb:hb + 1, :] - ecs[:, hb:hb + 1])
            dtf_row, dtb_row = dt_t[h:h + 1, :], dt_t[hb:hb + 1, :]
            dsel = jnp.where(strict_lower, dtf_row, jnp.where(strict_upper, dtb_row, dtf_row + dtb_row))
            m_mats.append((cb * jnp.exp(e_mat) * dsel).astype(BF16))
            x_blocks.append(jnp.where(head_of_lane == r, x_g, jnp.zeros_like(x_g)))
            e_cols.append(jnp.exp(csf_col))
            w_cols.append(dt[:, h:h + 1] * jnp.exp(tot[:, h:h + 1] - csf_col))
            decs.append(jnp.exp(tot[:, h:h + 1]))
        y_diag = jnp.dot(jnp.concatenate(m_mats, axis=1), jnp.concatenate(x_blocks, axis=0),
                         preferred_element_type=F32)
        xw_tiles, dec_tiles = [], []
        for pr in range(2):
            tile = g * 2 + pr
            sl = slice(pr * LANES, (pr + 1) * LANES)
            e_pair = _lane_pair(e_cols[2 * pr], e_cols[2 * pr + 1], lo_half)
            y_ref[:, tile * LANES:(tile + 1) * LANES] = y_diag[:, sl] + e_pair * y_off[:, sl]
            w_pair = _lane_pair(w_cols[2 * pr], w_cols[2 * pr + 1], lo_half)
            xw_tiles.append((x_g[:, sl].astype(F32) * w_pair).astype(BF16))
            dec_tiles.append(_lane_pair(decs[2 * pr], decs[2 * pr + 1], lo_half[0:1, :]))
        xw = jnp.concatenate(xw_tiles, axis=1)
        dec = jnp.concatenate(dec_tiles, axis=1)
        upd = lax.dot_general(b_g, xw, (((0,), (0,)), ((), ())), preferred_element_type=F32)
        state_scr[g] = st * dec + upd


def _ssd_bwd_kernel(xbc_ref, dk_tok_ref, dk_meta_ref, z_tok_ref, z_meta_ref, yf_ref,
                    dskip_ref, ng_ref, y_tok_ref, y_meta_ref, state_scr, y_scr, *, nc):
    i = pl.program_id(1)
    c = nc - 1 - i

    @pl.when(i == 0)
    def _():
        state_scr[...] = jnp.zeros_like(state_scr)

    dk = _decay_tiles(c, dk_tok_ref, dk_meta_ref)
    dt, ecs = dk(DK_DT), dk(DK_ECS)
    lo_half = lax.broadcasted_iota(jnp.int32, (CHUNK, LANES), 1) < SSD_HEADDIM
    tot = dk(DK_CS)[CHUNK - 1:CHUNK, :]

    for g in range(SSD_GROUPS):
        b_g = xbc_ref[:, D_INNER + g * D_STATE:D_INNER + (g + 1) * D_STATE]
        c_g = xbc_ref[:, D_INNER + SSD_GROUPS * D_STATE + g * D_STATE:
                      D_INNER + SSD_GROUPS * D_STATE + (g + 1) * D_STATE]
        st = state_scr[g]
        y_off = jnp.dot(c_g, st.astype(BF16), preferred_element_type=F32)
        xw_tiles, dec_tiles = [], []
        for pr in range(2):
            tile = g * 2 + pr
            x_pair = xbc_ref[:, tile * LANES:(tile + 1) * LANES].astype(F32)
            e_cols, w_cols, decs = [], [], []
            for hh in range(2):
                hb = SSD_HEADS + g * 4 + pr * 2 + hh
                ecs_col = ecs[:, hb:hb + 1]
                e_cols.append(jnp.exp(tot[:, hb:hb + 1] - ecs_col))
                w_cols.append(dt[:, hb:hb + 1] * jnp.exp(ecs_col))
                decs.append(jnp.exp(tot[:, hb:hb + 1]))
            sl = slice(tile * LANES, (tile + 1) * LANES)
            y_pair = (yf_ref[:, sl] + _lane_pair(e_cols[0], e_cols[1], lo_half) * y_off[:, pr * LANES:(pr + 1) * LANES]
                      + dskip_ref[:, sl] * x_pair)
            z_pair = jnp.where(c == 0, z_meta_ref[:, sl], z_tok_ref[:, sl]).astype(F32)
            y_scr[:, sl] = y_pair * (z_pair * _sigmoid(z_pair))
            xw_tiles.append((x_pair * _lane_pair(w_cols[0], w_cols[1], lo_half)).astype(BF16))
            dec_tiles.append(_lane_pair(decs[0], decs[1], lo_half[0:1, :]))
        xw = jnp.concatenate(xw_tiles, axis=1)
        dec = jnp.concatenate(dec_tiles, axis=1)
        upd = lax.dot_general(b_g, xw, (((0,), (0,)), ((), ())), preferred_element_type=F32)
        state_scr[g] = st * dec + upd

    gw = D_INNER // SSD_GROUPS
    for g in range(SSD_GROUPS):
        yg = y_scr[:, g * gw:(g + 1) * gw]
        yg = yg * lax.rsqrt(jnp.mean(yg * yg, axis=-1, keepdims=True) + EPS) * ng_ref[:, g * gw:(g + 1) * gw]
        y_scr[:, g * gw:(g + 1) * gw] = yg

    @pl.when(c > 0)
    def _():
        y_tok_ref[...] = y_scr[...].astype(BF16)

    @pl.when(c == 0)
    def _():
        y_meta_ref[...] = y_scr[...].astype(BF16)


def _ssd(dk_tok, dk_meta, proj_tok, proj_meta, conv_w, conv_b, dskip, ng, bsz, T):
    nct = T // CHUNK
    nc = nct + 1
    hb = CHUNK // HALO
    small = lambda b, c: (0, 0)
    dk_blk = (1, DK_N, CHUNK, LANES)
    dk_first = lambda b, c: (0, 0, 0, 0)
    state = pltpu.VMEM((SSD_GROUPS, D_STATE, 4 * SSD_HEADDIM), F32)
    shift = _conv_shift_matrix()
    xbc, y_f = pl.pallas_call(
        functools.partial(_ssd_fwd_kernel, nc=nc),
        grid=(bsz, nc),
        in_specs=[
            pl.BlockSpec((CHUNK, D_XBC), lambda b, c: (b * nct + jnp.maximum(c - 1, 0), 0)),
            pl.BlockSpec((HALO, D_XBC), lambda b, c: (jnp.maximum(b * nct * hb + (c - 1) * hb - 1, 0), 0)),
            pl.BlockSpec((HALO, D_XBC), lambda b, c: (b * nct * hb + jnp.minimum(c, nct - 1) * hb, 0)),
            pl.BlockSpec((CHUNK, D_XBC), small),
            pl.BlockSpec(shift.shape, small),
            pl.BlockSpec((8, D_XBC), small),
            pl.BlockSpec((1, D_XBC), small),
            pl.BlockSpec(dk_blk, lambda b, c: (b * nct + jnp.maximum(c - 1, 0), 0, 0, 0)),
            pl.BlockSpec(dk_blk, dk_first),
        ],
        out_specs=[
            pl.BlockSpec((CHUNK, D_XBC), lambda b, c: (b * nc + c, 0)),
            pl.BlockSpec((CHUNK, D_INNER), lambda b, c: (b * nc + c, 0)),
        ],
        out_shape=[
            jax.ShapeDtypeStruct((bsz * nc * CHUNK, D_XBC), BF16),
            jax.ShapeDtypeStruct((bsz * nc * CHUNK, D_INNER), F32),
        ],
        scratch_shapes=[state],
        compiler_params=_cparams(("parallel", "arbitrary")),
        name="ssd_fwd",
    )(proj_tok, proj_tok, proj_tok, proj_meta, shift, conv_w, conv_b, dk_tok, dk_meta)

    zb = COL_Z // D_INNER
    rev = lambda b, i: nc - 1 - i
    tok_blk = lambda b, i: b * nct + jnp.maximum(rev(b, i) - 1, 0)
    y_tok, y_meta = pl.pallas_call(
        functools.partial(_ssd_bwd_kernel, nc=nc),
        grid=(bsz, nc),
        in_specs=[
            pl.BlockSpec((CHUNK, D_XBC), lambda b, i: (b * nc + rev(b, i), 0)),
            pl.BlockSpec(dk_blk, lambda b, i: (tok_blk(b, i), 0, 0, 0)),
            pl.BlockSpec(dk_blk, dk_first),
            pl.BlockSpec((CHUNK, D_INNER), lambda b, i: (tok_blk(b, i), zb)),
            pl.BlockSpec((CHUNK, D_INNER), lambda b, i: (0, zb)),
            pl.BlockSpec((CHUNK, D_INNER), lambda b, i: (b * nc + rev(b, i), 0)),
            pl.BlockSpec((1, D_INNER), small),
            pl.BlockSpec((1, D_INNER), small),
        ],
        out_specs=[
            pl.BlockSpec((CHUNK, D_INNER), lambda b, i: (tok_blk(b, i), 0)),
            pl.BlockSpec((CHUNK, D_INNER), lambda b, i: (b, 0)),
        ],
        out_shape=[
            jax.ShapeDtypeStruct((bsz * T, D_INNER), BF16),
            jax.ShapeDtypeStruct((bsz * CHUNK, D_INNER), BF16),
        ],
        scratch_shapes=[state, pltpu.VMEM((CHUNK, D_INNER), F32)],
        compiler_params=_cparams(("parallel", "arbitrary")),
        name="ssd_bwd",
    )(xbc, dk_tok, dk_meta, proj_tok, proj_meta, y_f, dskip, ng)
    return y_tok, y_meta


def _head_rmsnorm(x, gain, lo_half):
    sq = x * x
    s_lo = jnp.sum(jnp.where(lo_half, sq, 0.0), axis=-1, keepdims=True)
    s_all = jnp.sum(sq, axis=-1, keepdims=True)
    r_lo = lax.rsqrt(s_lo * (1.0 / NA_HEADDIM) + EPS)
    r_hi = lax.rsqrt((s_all - s_lo) * (1.0 / NA_HEADDIM) + EPS)
    return x * jnp.where(lo_half, r_lo, r_hi) * gain


def _na_kernel(q_ref, kn_ref, vn_ref, km_ref, vm_ref, tb_ref, mb_ref, out_ref,
               k_scr, v_scr, s_scr, p_scr, l_scr, *, grid_rows):
    j = pl.program_id(2)
    lo_half = lax.broadcasted_iota(jnp.int32, (1, LANES), 1) < NA_HEADDIM
    step_lanes = NA_PAIRS_PER_STEP * LANES

    @pl.when(j == 0)
    def _():
        for scr in (k_scr, v_scr):
            scr[0:2 * NA_BLOCK, :] = jnp.zeros((2 * NA_BLOCK, step_lanes), BF16)

    @pl.when(j > 0)
    def _():
        for scr in (k_scr, v_scr):
            scr[0:NA_BLOCK, :] = scr[NA_BLOCK:2 * NA_BLOCK, :]
            scr[NA_BLOCK:2 * NA_BLOCK, :] = scr[2 * NA_BLOCK:3 * NA_BLOCK, :]

    k_scr[2 * NA_BLOCK:, :] = kn_ref[...]
    v_scr[2 * NA_BLOCK:, :] = vn_ref[...]

    @pl.when(j > 0)
    def _():
        jq = j - 1
        nt = (((1,), (1,)), ((), ()))
        n_keys = WIN_H * GRID_W
        units = [(a, pp) for a in range(NA_ROWS_PER_BLOCK) for pp in range(NA_PAIRS_PER_STEP)]
        offs, d0s = [], []
        for a in range(NA_ROWS_PER_BLOCK):
            r = NA_ROWS_PER_BLOCK * jq + a
            rstart = jnp.clip(r - WIN_H // 2, 0, grid_rows - WIN_H)
            offs.append(pl.multiple_of((rstart - NA_ROWS_PER_BLOCK * (jq - 1)) * GRID_W, GRID_W))
            d0s.append(rstart - r + (WIN_H - 1))
        for u, (a, pp) in enumerate(units):
            lanes = slice(pp * LANES, (pp + 1) * LANES)
            q_r = q_ref[a * GRID_W:(a + 1) * GRID_W, lanes]
            zero = jnp.zeros_like(q_r)
            q2 = jnp.concatenate([jnp.where(lo_half, q_r, zero), jnp.where(lo_half, zero, q_r)], axis=0)
            s_scr[u, :, 0:n_keys] = (
                lax.dot_general(q2, k_scr[pl.ds(offs[a], n_keys), lanes], nt, preferred_element_type=F32)
                + tb_ref[pp, d0s[a]])
            s_scr[u, :, n_keys:] = (lax.dot_general(q2, km_ref[:, lanes], nt, preferred_element_type=F32)
                                    + mb_ref[pp])
        for u in range(len(units)):
            s = s_scr[u]
            p = jnp.exp(s - jnp.max(s, axis=-1, keepdims=True))
            l_scr[u] = jnp.sum(p, axis=-1, keepdims=True)
            p_scr[u] = p.astype(BF16)
        for u, (a, pp) in enumerate(units):
            lanes = slice(pp * LANES, (pp + 1) * LANES)
            o = (jnp.dot(p_scr[u, :, 0:n_keys], v_scr[pl.ds(offs[a], n_keys), lanes], preferred_element_type=F32)
                 + jnp.dot(p_scr[u, :, n_keys:], vm_ref[:, lanes], preferred_element_type=F32)) / l_scr[u]
            out_ref[a * GRID_W:(a + 1) * GRID_W, lanes] = (
                jnp.where(lo_half, o[0:GRID_W], o[GRID_W:]).astype(BF16))


def _na_meta_kernel(q_ref, k_ref, v_ref, mb_ref, out_ref):
    lo_half = lax.broadcasted_iota(jnp.int32, (1, LANES), 1) < NA_HEADDIM
    k_meta = k_ref[META_PAD:, :]
    v_meta = v_ref[META_PAD:, :]
    q = q_ref[...]
    nt = (((1,), (1,)), ((), ()))
    outs = []
    for hh in range(2):
        q_h = jnp.where(lo_half == (hh == 0), q, jnp.zeros_like(q))
        s = lax.dot_general(q_h, k_meta, nt, preferred_element_type=F32) + mb_ref[0, hh:hh + 1, 0:N_META]
        p = jnp.exp(s - jnp.max(s, axis=-1, keepdims=True))
        o = jnp.dot(p.astype(BF16), v_meta, preferred_element_type=F32) / jnp.sum(p, axis=-1, keepdims=True)
        outs.append(o)
    out_ref[...] = jnp.where(lo_half, outs[0], outs[1]).astype(BF16)


def _na(qk_tok, proj_tok, qk_meta, proj_meta, tb, mb2, bsz, T):
    nb = T // NA_BLOCK
    step_lanes = NA_PAIRS_PER_STEP * LANES
    qc, kc, vc = QK_COL_Q // step_lanes, QK_COL_K // step_lanes, COL_V // step_lanes
    blk = (NA_BLOCK, step_lanes)
    q_blk = lambda p, b, j: (b * nb + jnp.maximum(j - 1, 0), qc + p)
    kv_blk = lambda col: (lambda p, b, j: (b * nb + jnp.minimum(j, nb - 1), col + p))
    two_heads = 2 * GRID_W
    n_keys = WIN_H * GRID_W + CHUNK
    units = NA_ROWS_PER_BLOCK * NA_PAIRS_PER_STEP
    return pl.pallas_call(
        functools.partial(_na_kernel, grid_rows=T // GRID_W),
        grid=(HEAD_PAIRS_NA // NA_PAIRS_PER_STEP, bsz, nb + 1),
        in_specs=[
            pl.BlockSpec(blk, q_blk),
            pl.BlockSpec(blk, kv_blk(kc)),
            pl.BlockSpec(blk, kv_blk(vc)),
            pl.BlockSpec((CHUNK, step_lanes), lambda p, b, j: (0, kc + p)),
            pl.BlockSpec((CHUNK, step_lanes), lambda p, b, j: (0, vc + p)),
            pl.BlockSpec((NA_PAIRS_PER_STEP, WIN_H, two_heads, WIN_H * GRID_W), lambda p, b, j: (p, 0, 0, 0)),
            pl.BlockSpec((NA_PAIRS_PER_STEP, two_heads, LANES), lambda p, b, j: (p, 0, 0)),
        ],
        out_specs=pl.BlockSpec(blk, lambda p, b, j: (b * nb + jnp.maximum(j - 1, 0), p)),
        out_shape=jax.ShapeDtypeStruct((bsz * T, D_NA), BF16),
        scratch_shapes=[
            pltpu.VMEM((3 * NA_BLOCK, step_lanes), BF16),
            pltpu.VMEM((3 * NA_BLOCK, step_lanes), BF16),
            pltpu.VMEM((units, two_heads, n_keys), F32),
            pltpu.VMEM((units, two_heads, n_keys), BF16),
            pltpu.VMEM((units, two_heads, 1), F32),
        ],
        compiler_params=_cparams(("arbitrary", "arbitrary", "arbitrary")),
        name="natten",
    )(qk_tok, qk_tok, proj_tok, qk_meta, proj_meta, tb, mb2)


def _na_meta(qk_meta, proj_meta, mb):
    qc, kc, vc = QK_COL_Q // LANES, QK_COL_K // LANES, COL_V // LANES
    blk = (CHUNK, LANES)
    return pl.pallas_call(
        _na_meta_kernel,
        grid=(HEAD_PAIRS_NA,),
        in_specs=[
            pl.BlockSpec(blk, lambda p: (0, qc + p)),
            pl.BlockSpec(blk, lambda p: (0, kc + p)),
            pl.BlockSpec(blk, lambda p: (0, vc + p)),
            pl.BlockSpec((1, 8, LANES), lambda p: (p, 0, 0)),
        ],
        out_specs=pl.BlockSpec(blk, lambda p: (0, p)),
        out_shape=jax.ShapeDtypeStruct((CHUNK, D_NA), BF16),
        compiler_params=_cparams(("arbitrary",)),
        name="natten_meta",
    )(qk_meta, qk_meta, proj_meta, mb)


def _rel_bias_table(rel_bias):
    w = np.arange(GRID_W)
    cstart = np.clip(w - WIN_W // 2, 0, GRID_W - WIN_W)
    cc = np.arange(GRID_W)
    inside = (cc[None, :] >= cstart[:, None]) & (cc[None, :] < cstart[:, None] + WIN_W)
    t = np.arange(2 * WIN_W - 1)
    onehot = ((cc[None, None, :] - w[None, :, None] + (WIN_W - 1)) == t[:, None, None]) & inside[None]
    rows = jnp.stack([rel_bias[:, d0:d0 + WIN_H] for d0 in range(WIN_H)], axis=1)
    rows = rows.reshape(HEAD_PAIRS_NA, 2, WIN_H, WIN_H, 2 * WIN_W - 1)
    g = jnp.einsum("phdit,twc->pdhwic", rows, jnp.asarray(onehot, F32), precision=lax.Precision.HIGHEST)
    g = jnp.where(inside[None, None, None, :, None, :], g, NEG_BIG)
    return g.reshape(HEAD_PAIRS_NA, WIN_H, 2 * GRID_W, WIN_H * GRID_W)


def _merge_kernel(ys_ref, yn_ref, gs_ref, gn_ref, h_ref, wbs_ref, wbn_ref, wo_ref, out_ref):
    m_s = jnp.dot(ys_ref[...], wbs_ref[...], preferred_element_type=F32)
    m_n = jnp.dot(yn_ref[...], wbn_ref[...], preferred_element_type=F32)
    merged = _sigmoid(gs_ref[...].astype(F32)) * m_s + _sigmoid(gn_ref[...].astype(F32)) * m_n
    out_ref[...] = h_ref[...] + jnp.dot(merged.astype(BF16), wo_ref[...], preferred_element_type=F32)


def _merge(y_ssd, y_na, proj, h, wbs, wbn, wo, tm, shared_rows):
    rows = y_ssd.shape[0]
    gsc, gnc = COL_GS // D_MODEL, COL_GN // D_MODEL
    r = (lambda i: 0) if shared_rows else (lambda i: i)
    full = lambda i: (0, 0)
    return pl.pallas_call(
        _merge_kernel,
        grid=(rows // tm,),
        in_specs=[
            pl.BlockSpec((tm, D_INNER), lambda i: (i, 0)),
            pl.BlockSpec((tm, D_NA), lambda i: (r(i), 0)),
            pl.BlockSpec((tm, D_MODEL), lambda i: (r(i), gsc)),
            pl.BlockSpec((tm, D_MODEL), lambda i: (r(i), gnc)),
            pl.BlockSpec((tm, D_MODEL), lambda i: (r(i), 0)),
            pl.BlockSpec((D_INNER, D_MODEL), full),
            pl.BlockSpec((D_NA, D_MODEL), full),
            pl.BlockSpec((D_MODEL, D_MODEL), full),
        ],
        out_specs=pl.BlockSpec((tm, D_MODEL), lambda i: (i, 0)),
        out_shape=jax.ShapeDtypeStruct((rows, D_MODEL), F32),
        compiler_params=_cparams(("parallel",)),
        name="merge",
    )(y_ssd, y_na, proj, proj, h, wbs, wbn, wo)


FFN_HALO = 16
FFN_TF = 256


def _ffn_kernel(h_ref, prev_ref, next_ref, meta_ref, g_ref, wup_ref, cw_ref, cb_ref, wd_ref, out_ref,
                u_scr, act_scr, *, tiles_per_seq, tm):
    pos = pl.program_id(0) % tiles_per_seq
    gain = g_ref[...]
    prev = jnp.where(pos == 0, meta_ref[...], prev_ref[...])
    nxt = jnp.where(pos == tiles_per_seq - 1, 0.0, next_ref[...])
    u_scr[0:FFN_HALO, :] = _rmsnorm_rows(prev, gain).astype(BF16)
    u_scr[FFN_HALO:FFN_HALO + tm, :] = _rmsnorm_rows(h_ref[...], gain).astype(BF16)
    u_scr[FFN_HALO + tm:, :] = _rmsnorm_rows(nxt, gain).astype(BF16)
    u = u_scr[...]

    n_ext = tm + 2 * FFN_HALO

    def conv(x_ext, cols):
        y = cw_ref[1:2, cols] * x_ext + cb_ref[:, cols]
        y = y + pltpu.roll(cw_ref[0:1, cols] * x_ext, 1, 0)
        y = y + pltpu.roll(cw_ref[2:3, cols] * x_ext, n_ext - 1, 0)
        return y[FFN_HALO:FFN_HALO + tm]

    for k in range(D_FF // FFN_TF):
        cols_a = slice(k * FFN_TF, (k + 1) * FFN_TF)
        cols_g = slice(D_FF + k * FFN_TF, D_FF + (k + 1) * FFN_TF)
        a = conv(jnp.dot(u, wup_ref[:, cols_a], preferred_element_type=F32), cols_a)
        g = conv(jnp.dot(u, wup_ref[:, cols_g], preferred_element_type=F32), cols_g)
        act_scr[:, cols_a] = (g * _sigmoid(g) * a).astype(BF16)
    out_ref[...] = h_ref[...] + jnp.dot(act_scr[...], wd_ref[...], preferred_element_type=F32)


def _ffn(h1, h1_meta, g_ffn, w_up, cw, cb, w_down, bsz, T, tm):
    rows = bsz * T
    tps = T // tm
    hb = tm // FFN_HALO
    mb = CHUNK // FFN_HALO
    nblk = rows // FFN_HALO
    const = lambda i: (0, 0)
    resident = pl.Buffered(1)
    return pl.pallas_call(
        functools.partial(_ffn_kernel, tiles_per_seq=tps, tm=tm),
        grid=(rows // tm,),
        in_specs=[
            pl.BlockSpec((tm, D_MODEL), lambda i: (i, 0)),
            pl.BlockSpec((FFN_HALO, D_MODEL), lambda i: (jnp.maximum(i * hb - 1, 0), 0)),
            pl.BlockSpec((FFN_HALO, D_MODEL), lambda i: (jnp.minimum((i + 1) * hb, nblk - 1), 0)),
            pl.BlockSpec((FFN_HALO, D_MODEL), lambda i: ((i // tps) * mb + mb - 1, 0)),
            pl.BlockSpec((1, D_MODEL), const),
            pl.BlockSpec((D_MODEL, 2 * D_FF), const, pipeline_mode=resident),
            pl.BlockSpec((8, 2 * D_FF), const),
            pl.BlockSpec((1, 2 * D_FF), const),
            pl.BlockSpec((D_FF, D_MODEL), const, pipeline_mode=resident),
        ],
        out_specs=pl.BlockSpec((tm, D_MODEL), lambda i: (i, 0)),
        out_shape=jax.ShapeDtypeStruct((rows, D_MODEL), F32),
        scratch_shapes=[
            pltpu.VMEM((tm + 2 * FFN_HALO, D_MODEL), BF16),
            pltpu.VMEM((tm, D_FF), BF16),
        ],
        compiler_params=_cparams(("parallel",)),
        name="ffn",
    )(h1, h1, h1, h1_meta, g_ffn, w_up, cw, cb, w_down)


def _pad_rows(a, n):
    return jnp.pad(a, ((0, n - a.shape[0]), (0, 0)))


def _row_tile(rows, cap):
    tm = min(cap, rows)
    assert rows % tm == 0
    return tm


def kernel(x_prompt, x_sample, meta_tokens, g_mix, w_in, ssd_conv_w, ssd_conv_b, dt_bias_f, dt_bias_b,
           a_log_f, a_log_b, d_skip, ssd_norm_g, q_norm_g, k_norm_g, rel_bias, meta_bias, w_br_ssd,
           w_br_na, w_out, g_ffn, w_up, ffn_conv_w, ffn_conv_b, w_down):
    assert g_mix.shape[0] == 1, "single-layer block"
    w = w_in[0]
    o_z, o_xbc = 0, D_INNER
    o_dtf = o_xbc + D_XBC
    o_q = o_dtf + 2 * SSD_HEADS
    o_v = o_q + 2 * D_NA
    w_main = jnp.concatenate([w[:, o_xbc:o_dtf], w[:, o_z:o_xbc], w[:, o_v:]], axis=1).astype(BF16)
    w_qk = w[:, o_q:o_v].astype(BF16)
    w_dt = jnp.pad(w[:, o_dtf:o_q], ((0, 0), (0, LANES - 2 * SSD_HEADS))).astype(BF16)
    g_mix2 = g_mix.astype(F32)
    lane_pad = lambda v: jnp.pad(v, (0, LANES - v.shape[0]))[None, :].astype(F32)
    dtb = lane_pad(jnp.concatenate([dt_bias_f[0], dt_bias_b[0]]))
    alog = lane_pad(jnp.concatenate([a_log_f[0], a_log_b[0]]))
    dskip = jnp.repeat(d_skip[0].astype(F32), SSD_HEADDIM)[None, :]
    ng = ssd_norm_g.astype(F32)
    conv_w = _pad_rows(ssd_conv_w[0].astype(F32), 8)
    conv_b = ssd_conv_b.astype(F32)
    qg = jnp.tile(q_norm_g[0].astype(F32), 2)[None, :] * (NA_HEADDIM ** -0.5)
    kg = jnp.tile(k_norm_g[0].astype(F32), 2)[None, :]
    qk_gain = jnp.stack([qg, kg])
    tb = _rel_bias_table(rel_bias[0].astype(F32))
    mb = jnp.pad(meta_bias[0].astype(F32).reshape(HEAD_PAIRS_NA, 2, N_META), ((0, 0), (0, 6), (0, LANES - N_META)))
    mb2 = jnp.repeat(meta_bias[0].astype(F32).reshape(HEAD_PAIRS_NA, 2, N_META), GRID_W, axis=1)
    mb2 = jnp.pad(mb2, ((0, 0), (0, 0), (META_PAD, 0)), constant_values=NEG_BIG)
    wbs, wbn, wo = w_br_ssd[0].astype(BF16), w_br_na[0].astype(BF16), w_out[0].astype(BF16)
    w_up_b, w_down_b = w_up[0].astype(BF16), w_down[0].astype(BF16)
    ffn_cw = _pad_rows(ffn_conv_w[0].astype(F32), 8)
    ffn_cb = ffn_conv_b.astype(F32)

    h_meta = jnp.concatenate([jnp.zeros((META_PAD, D_MODEL), F32), meta_tokens.astype(F32)], axis=0)
    proj_meta, dt_meta, qk_meta = _inproj(h_meta, g_mix2, w_main, w_dt, w_qk, qk_gain, CHUNK)
    dk_meta = _dt_prep(dt_meta, dtb, alog, 1, True)
    y_na_meta = _na_meta(qk_meta, proj_meta, mb)

    def run(x):
        bsz, T, _ = x.shape
        assert T % NA_BLOCK == 0 and T % CHUNK == 0
        x2d = x.reshape(bsz * T, D_MODEL)
        tm = _row_tile(bsz * T, 1024)
        proj_tok, dt_tok, qk_tok = _inproj(x2d, g_mix2, w_main, w_dt, w_qk, qk_gain, _row_tile(bsz * T, 2048))
        dk_tok = _dt_prep(dt_tok, dtb, alog, DT_PREP_CHUNKS, False)
        y_ssd, y_ssd_meta = _ssd(dk_tok, dk_meta, proj_tok, proj_meta, conv_w, conv_b, dskip, ng, bsz, T)
        y_na = _na(qk_tok, proj_tok, qk_meta, proj_meta, tb, mb2, bsz, T)
        h1 = _merge(y_ssd, y_na, proj_tok, x2d, wbs, wbn, wo, tm, False)
        h1_meta = _merge(y_ssd_meta, y_na_meta, proj_meta, h_meta, wbs, wbn, wo, CHUNK, True)
        tf = _row_tile(T, 1024)
        out = _ffn(h1, h1_meta, g_ffn.astype(F32), w_up_b, ffn_cw, ffn_cb, w_down_b, bsz, T, tf)
        return out.reshape(bsz, T, D_MODEL)

    return (run(x_prompt), run(x_sample))
```

```python
import functools

import numpy as np
import jax
import jax.numpy as jnp
from jax import lax
from jax.experimental import pallas as pl
from jax.experimental.pallas import tpu as pltpu

F32 = jnp.float32
BF16 = jnp.bfloat16

D_MODEL = 1024
N_META = 16
GRID_W = 64
D_INNER = 2048
SSD_HEADDIM = 64
SSD_HEADS = 32
SSD_GROUPS = 8
D_STATE = 128
D_CONV = 5
CHUNK = 128
D_XBC = D_INNER + 2 * SSD_GROUPS * D_STATE
NA_HEADS = 16
NA_HEADDIM = 64
D_NA = NA_HEADS * NA_HEADDIM
WIN_H = 8
WIN_W = 16
D_FF = 2816
FFN_CONV = 3
EPS = 1e-6

LANES = 128
META_PAD = CHUNK - N_META
HEAD_PAIRS_SSD = SSD_HEADS // 2
HEAD_PAIRS_NA = NA_HEADS // 2
NA_ROWS_PER_BLOCK = 16
NA_BLOCK = NA_ROWS_PER_BLOCK * GRID_W
NA_PAIRS_PER_STEP = 2
NEG_BIG = -1e30

COL_XBC = 0
COL_Z = COL_XBC + D_XBC
COL_V = COL_Z + D_INNER
COL_GS = COL_V + D_NA
COL_GN = COL_GS + D_MODEL
D_PROJ = COL_GN + D_MODEL
QK_COL_Q = 0
QK_COL_K = D_NA

VMEM_LIMIT = 56 * 1024 * 1024


def _cparams(sem):
    return pltpu.CompilerParams(dimension_semantics=sem, vmem_limit_bytes=VMEM_LIMIT)


def _sigmoid(x):
    return 1.0 / (1.0 + jnp.exp(-x))


def _softplus(x):
    return jnp.maximum(x, 0.0) + jnp.log(1.0 + jnp.exp(-jnp.abs(x)))


def _rmsnorm_rows(x, g):
    ms = jnp.mean(x * x, axis=-1, keepdims=True)
    return x * lax.rsqrt(ms + EPS) * g


INPROJ_TN = 1024


def _inproj_kernel(x_ref, g_ref, w_ref, wdt_ref, out_ref, dt_ref, u_scr):
    @pl.when(pl.program_id(1) == 0)
    def _():
        u = _rmsnorm_rows(x_ref[...], g_ref[...]).astype(BF16)
        u_scr[...] = u
        dt_ref[...] = jnp.dot(u, wdt_ref[...], preferred_element_type=F32)

    out_ref[...] = jnp.dot(u_scr[...], w_ref[...], preferred_element_type=F32).astype(BF16)


def _inproj_qk_kernel(x_ref, g_ref, w_ref, gain_ref, out_ref, u_scr):
    @pl.when(pl.program_id(1) == 0)
    def _():
        u_scr[...] = _rmsnorm_rows(x_ref[...], g_ref[...]).astype(BF16)

    res = jnp.dot(u_scr[...], w_ref[...], preferred_element_type=F32)
    lo_half = lax.broadcasted_iota(jnp.int32, (1, LANES), 1) < NA_HEADDIM
    gain = gain_ref[0]
    for t in range(HEAD_PAIRS_NA):
        cols = slice(t * LANES, (t + 1) * LANES)
        out_ref[:, cols] = _head_rmsnorm(res[:, cols], gain, lo_half).astype(BF16)


def _inproj(x2d, g_mix, w_main, w_dt, w_qk, qk_gain, tm):
    rows = x2d.shape[0]
    tn = INPROJ_TN
    row_blk = pl.BlockSpec((tm, D_MODEL), lambda i, j: (i, 0))
    gain_blk = pl.BlockSpec((1, D_MODEL), lambda i, j: (0, 0))
    proj, dt = pl.pallas_call(
        _inproj_kernel,
        grid=(rows // tm, D_PROJ // tn),
        in_specs=[
            row_blk,
            gain_blk,
            pl.BlockSpec((D_MODEL, tn), lambda i, j: (0, j)),
            pl.BlockSpec((D_MODEL, LANES), lambda i, j: (0, 0)),
        ],
        out_specs=[
            pl.BlockSpec((tm, tn), lambda i, j: (i, j)),
            pl.BlockSpec((tm, LANES), lambda i, j: (i, 0)),
        ],
        out_shape=[
            jax.ShapeDtypeStruct((rows, D_PROJ), BF16),
            jax.ShapeDtypeStruct((rows, LANES), F32),
        ],
        scratch_shapes=[pltpu.VMEM((tm, D_MODEL), BF16)],
        compiler_params=_cparams(("parallel", "arbitrary")),
        name="inproj",
    )(x2d, g_mix, w_main, w_dt)
    qk = pl.pallas_call(
        _inproj_qk_kernel,
        grid=(rows // tm, 2),
        in_specs=[
            row_blk,
            gain_blk,
            pl.BlockSpec((D_MODEL, D_NA), lambda i, j: (0, j)),
            pl.BlockSpec((1, 1, LANES), lambda i, j: (j, 0, 0)),
        ],
        out_specs=pl.BlockSpec((tm, D_NA), lambda i, j: (i, j)),
        out_shape=jax.ShapeDtypeStruct((rows, 2 * D_NA), BF16),
        scratch_shapes=[pltpu.VMEM((tm, D_MODEL), BF16)],
        compiler_params=_cparams(("parallel", "arbitrary")),
        name="inproj_qk",
    )(x2d, g_mix, w_qk, qk_gain)
    return proj, dt, qk


HALO = 16


CONV_PAD = (D_CONV - 1) // 2
CONV_SIDE_TAPS = tuple(k for k in range(D_CONV) if k != CONV_PAD)


def _conv_shift_matrix():
    sh = np.zeros((len(CONV_SIDE_TAPS) * CHUNK, CHUNK + 2 * HALO), np.float32)
    r = np.arange(CHUNK)
    for j, k in enumerate(CONV_SIDE_TAPS):
        sh[j * CHUNK + r, HALO + r + k - CONV_PAD] = 1.0
    return jnp.asarray(sh, BF16)


def _conv_silu_chunk(c, nc, tok_ref, prev_ref, next_ref, meta_ref, shift_ref, w_ref, b_ref, out_ref):
    is_meta = c == 0
    strip = 2 * LANES
    for s in range(D_XBC // strip):
        cols = slice(s * strip, (s + 1) * strip)
        main = jnp.where(is_meta, meta_ref[:, cols], tok_ref[:, cols])
        prev = jnp.where(c == 1, meta_ref[CHUNK - HALO:, cols], prev_ref[:, cols])
        prev = jnp.where(is_meta, jnp.zeros_like(prev), prev)
        nxt = jnp.where(c == nc - 1, jnp.zeros_like(prev), next_ref[:, cols])
        ext = jnp.concatenate([prev, main, nxt], axis=0)
        acc = w_ref[CONV_PAD:CONV_PAD + 1, cols] * main.astype(F32) + b_ref[:, cols]
        for j, k in enumerate(CONV_SIDE_TAPS):
            shifted = jnp.dot(shift_ref[j * CHUNK:(j + 1) * CHUNK, :], ext, preferred_element_type=F32)
            acc = acc + w_ref[k:k + 1, cols] * shifted
        out_ref[:, cols] = (acc * _sigmoid(acc)).astype(BF16)


DK_DT, DK_CS, DK_ECS, DK_CS_T, DK_ECS_T, DK_DT_T = range(6)
DK_N = 6
DT_PREP_CHUNKS = 16


def _dt_prep_kernel(dt_ref, dtb_ref, alog_ref, out_ref, *, n_chunks, zero_pad_rows):
    row = lax.broadcasted_iota(jnp.int32, (CHUNK, LANES), 0)
    col = lax.broadcasted_iota(jnp.int32, (CHUNK, LANES), 1)
    tri = (col <= row).astype(F32)
    neg_a = -jnp.exp(alog_ref[...])
    for i in range(n_chunks):
        dt = _softplus(dt_ref[i * CHUNK:(i + 1) * CHUNK, :] + dtb_ref[...])
        if zero_pad_rows:
            dt = jnp.where(row < META_PAD, 0.0, dt)
        a = dt * neg_a
        cs = jnp.dot(tri, a, preferred_element_type=F32, precision=lax.Precision.HIGHEST)
        ecs = cs - a
        out_ref[i, DK_DT] = dt
        out_ref[i, DK_CS] = cs
        out_ref[i, DK_ECS] = ecs
        out_ref[i, DK_CS_T] = cs.T
        out_ref[i, DK_ECS_T] = ecs.T
        out_ref[i, DK_DT_T] = dt.T


def _dt_prep(dt_raw, dtb, alog, chunks_per_step, zero_pad_rows):
    n = dt_raw.shape[0] // CHUNK
    assert n % chunks_per_step == 0
    small = lambda i: (0, 0)
    return pl.pallas_call(
        functools.partial(_dt_prep_kernel, n_chunks=chunks_per_step, zero_pad_rows=zero_pad_rows),
        grid=(n // chunks_per_step,),
        in_specs=[
            pl.BlockSpec((chunks_per_step * CHUNK, LANES), lambda i: (i, 0)),
            pl.BlockSpec((1, LANES), small),
            pl.BlockSpec((1, LANES), small),
        ],
        out_specs=pl.BlockSpec((chunks_per_step, DK_N, CHUNK, LANES), lambda i: (i, 0, 0, 0)),
        out_shape=jax.ShapeDtypeStruct((n, DK_N, CHUNK, LANES), F32),
        compiler_params=_cparams(("parallel",)),
        name="dt_prep",
    )(dt_raw, dtb, alog)


def _lane_pair(col_lo, col_hi, lo_mask):
    return jnp.where(lo_mask, col_lo, col_hi)


def _decay_tiles(c, dk_tok_ref, dk_meta_ref):
    return lambda k: jnp.where(c == 0, dk_meta_ref[0, k], dk_tok_ref[0, k])


def _ssd_fwd_kernel(tok_ref, prev_ref, next_ref, meta_ref, shift_ref, w_ref, b_ref, dk_tok_ref, dk_meta_ref,
                    xbc_ref, y_ref, state_scr, *, nc):
    c = pl.program_id(1)

    @pl.when(c == 0)
    def _():
        state_scr[...] = jnp.zeros_like(state_scr)

    _conv_silu_chunk(c, nc, tok_ref, prev_ref, next_ref, meta_ref, shift_ref, w_ref, b_ref, xbc_ref)

    dk = _decay_tiles(c, dk_tok_ref, dk_meta_ref)
    dt, cs, ecs, cs_t, ecs_t, dt_t = (dk(k) for k in (DK_DT, DK_CS, DK_ECS, DK_CS_T, DK_ECS_T, DK_DT_T))
    row = lax.broadcasted_iota(jnp.int32, (CHUNK, LANES), 0)
    col = lax.broadcasted_iota(jnp.int32, (CHUNK, LANES), 1)
    lower = row >= col
    strict_lower = row > col
    strict_upper = row < col
    lo_half = col < SSD_HEADDIM
    tot = cs[CHUNK - 1:CHUNK, :]
    heads_per_group = SSD_HEADS // SSD_GROUPS
    gw = heads_per_group * SSD_HEADDIM
    head_of_lane = lax.broadcasted_iota(jnp.int32, (CHUNK, gw), 1) // SSD_HEADDIM

    for g in range(SSD_GROUPS):
        b_g = xbc_ref[:, D_INNER + g * D_STATE:D_INNER + (g + 1) * D_STATE]
        c_g = xbc_ref[:, D_INNER + SSD_GROUPS * D_STATE + g * D_STATE:
                      D_INNER + SSD_GROUPS * D_STATE + (g + 1) * D_STATE]
        cb = lax.dot_general(c_g, b_g, (((1,), (1,)), ((), ())), preferred_element_type=F32)
        st = state_scr[g]
        y_off = jnp.dot(c_g, st.astype(BF16), preferred_element_type=F32)
        x_g = xbc_ref[:, g * gw:(g + 1) * gw]
        m_mats, x_blocks, e_cols, w_cols, decs = [], [], [], [], []
        for r in range(heads_per_group):
            h = g * heads_per_group + r
            hb = SSD_HEADS + h
            csf_col = cs[:, h:h + 1]
            e_mat = jnp.where(lower, csf_col - cs_t[h:h + 1, :], ecs_t[hb:hb + 1, :] - ecs[:, hb:hb + 1])
            dtf_row, dtb_row = dt_t[h:h + 1, :], dt_t[hb:hb + 1, :]
            dsel = jnp.where(strict_lower, dtf_row, jnp.where(strict_upper, dtb_row, dtf_row + dtb_row))
            m_mats.append((cb * jnp.exp(e_mat) * dsel).astype(BF16))
            x_blocks.append(jnp.where(head_of_lane == r, x_g, jnp.zeros_like(x_g)))
            e_cols.append(jnp.exp(csf_col))
            w_cols.append(dt[:, h:h + 1] * jnp.exp(tot[:, h:h + 1] - csf_col))
            decs.append(jnp.exp(tot[:, h:h + 1]))
        y_diag = jnp.dot(jnp.concatenate(m_mats, axis=1), jnp.concatenate(x_blocks, axis=0),
                         preferred_element_type=F32)
        xw_tiles, dec_tiles = [], []
        for pr in range(2):
            tile = g * 2 + pr
            sl = slice(pr * LANES, (pr + 1) * LANES)
            e_pair = _lane_pair(e_cols[2 * pr], e_cols[2 * pr + 1], lo_half)
            y_ref[:, tile * LANES:(tile + 1) * LANES] = y_diag[:, sl] + e_pair * y_off[:, sl]
            w_pair = _lane_pair(w_cols[2 * pr], w_cols[2 * pr + 1], lo_half)
            xw_tiles.append((x_g[:, sl].astype(F32) * w_pair).astype(BF16))
            dec_tiles.append(_lane_pair(decs[2 * pr], decs[2 * pr + 1], lo_half[0:1, :]))
        xw = jnp.concatenate(xw_tiles, axis=1)
        dec = jnp.concatenate(dec_tiles, axis=1)
        upd = lax.dot_general(b_g, xw, (((0,), (0,)), ((), ())), preferred_element_type=F32)
        state_scr[g] = st * dec + upd


def _ssd_bwd_kernel(xbc_ref, dk_tok_ref, dk_meta_ref, z_tok_ref, z_meta_ref, yf_ref,
                    dskip_ref, ng_ref, y_tok_ref, y_meta_ref, state_scr, y_scr, *, nc):
    i = pl.program_id(1)
    c = nc - 1 - i

    @pl.when(i == 0)
    def _():
        state_scr[...] = jnp.zeros_like(state_scr)

    dk = _decay_tiles(c, dk_tok_ref, dk_meta_ref)
    dt, ecs = dk(DK_DT), dk(DK_ECS)
    lo_half = lax.broadcasted_iota(jnp.int32, (CHUNK, LANES), 1) < SSD_HEADDIM
    tot = dk(DK_CS)[CHUNK - 1:CHUNK, :]

    for g in range(SSD_GROUPS):
        b_g = xbc_ref[:, D_INNER + g * D_STATE:D_INNER + (g + 1) * D_STATE]
        c_g = xbc_ref[:, D_INNER + SSD_GROUPS * D_STATE + g * D_STATE:
                      D_INNER + SSD_GROUPS * D_STATE + (g + 1) * D_STATE]
        st = state_scr[g]
        y_off = jnp.dot(c_g, st.astype(BF16), preferred_element_type=F32)
        xw_tiles, dec_tiles = [], []
        for pr in range(2):
            tile = g * 2 + pr
            x_pair = xbc_ref[:, tile * LANES:(tile + 1) * LANES].astype(F32)
            e_cols, w_cols, decs = [], [], []
            for hh in range(2):
                hb = SSD_HEADS + g * 4 + pr * 2 + hh
                ecs_col = ecs[:, hb:hb + 1]
                e_cols.append(jnp.exp(tot[:, hb:hb + 1] - ecs_col))
                w_cols.append(dt[:, hb:hb + 1] * jnp.exp(ecs_col))
                decs.append(jnp.exp(tot[:, hb:hb + 1]))
            sl = slice(tile * LANES, (tile + 1) * LANES)
            y_pair = (yf_ref[:, sl] + _lane_pair(e_cols[0], e_cols[1], lo_half) * y_off[:, pr * LANES:(pr + 1) * LANES]
                      + dskip_ref[:, sl] * x_pair)
            z_pair = jnp.where(c == 0, z_meta_ref[:, sl], z_tok_ref[:, sl]).astype(F32)
            y_scr[:, sl] = y_pair * (z_pair * _sigmoid(z_pair))
            xw_tiles.append((x_pair * _lane_pair(w_cols[0], w_cols[1], lo_half)).astype(BF16))
            dec_tiles.append(_lane_pair(decs[0], decs[1], lo_half[0:1, :]))
        xw = jnp.concatenate(xw_tiles, axis=1)
        dec = jnp.concatenate(dec_tiles, axis=1)
        upd = lax.dot_general(b_g, xw, (((0,), (0,)), ((), ())), preferred_element_type=F32)
        state_scr[g] = st * dec + upd

    gw = D_INNER // SSD_GROUPS
    for g in range(SSD_GROUPS):
        yg = y_scr[:, g * gw:(g + 1) * gw]
        yg = yg * lax.rsqrt(jnp.mean(yg * yg, axis=-1, keepdims=True) + EPS) * ng_ref[:, g * gw:(g + 1) * gw]
        y_scr[:, g * gw:(g + 1) * gw] = yg

    @pl.when(c > 0)
    def _():
        y_tok_ref[...] = y_scr[...].astype(BF16)

    @pl.when(c == 0)
    def _():
        y_meta_ref[...] = y_scr[...].astype(BF16)


def _ssd(dk_tok, dk_meta, proj_tok, proj_meta, conv_w, conv_b, dskip, ng, bsz, T):
    nct = T // CHUNK
    nc = nct + 1
    hb = CHUNK // HALO
    small = lambda b, c: (0, 0)
    dk_blk = (1, DK_N, CHUNK, LANES)
    dk_first = lambda b, c: (0, 0, 0, 0)
    state = pltpu.VMEM((SSD_GROUPS, D_STATE, 4 * SSD_HEADDIM), F32)
    shift = _conv_shift_matrix()
    xbc, y_f = pl.pallas_call(
        functools.partial(_ssd_fwd_kernel, nc=nc),
        grid=(bsz, nc),
        in_specs=[
            pl.BlockSpec((CHUNK, D_XBC), lambda b, c: (b * nct + jnp.maximum(c - 1, 0), 0)),
            pl.BlockSpec((HALO, D_XBC), lambda b, c: (jnp.maximum(b * nct * hb + (c - 1) * hb - 1, 0), 0)),
            pl.BlockSpec((HALO, D_XBC), lambda b, c: (b * nct * hb + jnp.minimum(c, nct - 1) * hb, 0)),
            pl.BlockSpec((CHUNK, D_XBC), small),
            pl.BlockSpec(shift.shape, small),
            pl.BlockSpec((8, D_XBC), small),
            pl.BlockSpec((1, D_XBC), small),
            pl.BlockSpec(dk_blk, lambda b, c: (b * nct + jnp.maximum(c - 1, 0), 0, 0, 0)),
            pl.BlockSpec(dk_blk, dk_first),
        ],
        out_specs=[
            pl.BlockSpec((CHUNK, D_XBC), lambda b, c: (b * nc + c, 0)),
            pl.BlockSpec((CHUNK, D_INNER), lambda b, c: (b * nc + c, 0)),
        ],
        out_shape=[
            jax.ShapeDtypeStruct((bsz * nc * CHUNK, D_XBC), BF16),
            jax.ShapeDtypeStruct((bsz * nc * CHUNK, D_INNER), F32),
        ],
        scratch_shapes=[state],
        compiler_params=_cparams(("parallel", "arbitrary")),
        name="ssd_fwd",
    )(proj_tok, proj_tok, proj_tok, proj_meta, shift, conv_w, conv_b, dk_tok, dk_meta)

    zb = COL_Z // D_INNER
    rev = lambda b, i: nc - 1 - i
    tok_blk = lambda b, i: b * nct + jnp.maximum(rev(b, i) - 1, 0)
    y_tok, y_meta = pl.pallas_call(
        functools.partial(_ssd_bwd_kernel, nc=nc),
        grid=(bsz, nc),
        in_specs=[
            pl.BlockSpec((CHUNK, D_XBC), lambda b, i: (b * nc + rev(b, i), 0)),
            pl.BlockSpec(dk_blk, lambda b, i: (tok_blk(b, i), 0, 0, 0)),
            pl.BlockSpec(dk_blk, dk_first),
            pl.BlockSpec((CHUNK, D_INNER), lambda b, i: (tok_blk(b, i), zb)),
            pl.BlockSpec((CHUNK, D_INNER), lambda b, i: (0, zb)),
            pl.BlockSpec((CHUNK, D_INNER), lambda b, i: (b * nc + rev(b, i), 0)),
            pl.BlockSpec((1, D_INNER), small),
            pl.BlockSpec((1, D_INNER), small),
        ],
        out_specs=[
            pl.BlockSpec((CHUNK, D_INNER), lambda b, i: (tok_blk(b, i), 0)),
            pl.BlockSpec((CHUNK, D_INNER), lambda b, i: (b, 0)),
        ],
        out_shape=[
            jax.ShapeDtypeStruct((bsz * T, D_INNER), BF16),
            jax.ShapeDtypeStruct((bsz * CHUNK, D_INNER), BF16),
        ],
        scratch_shapes=[state, pltpu.VMEM((CHUNK, D_INNER), F32)],
        compiler_params=_cparams(("parallel", "arbitrary")),
        name="ssd_bwd",
    )(xbc, dk_tok, dk_meta, proj_tok, proj_meta, y_f, dskip, ng)
    return y_tok, y_meta


def _head_rmsnorm(x, gain, lo_half):
    sq = x * x
    s_lo = jnp.sum(jnp.where(lo_half, sq, 0.0), axis=-1, keepdims=True)
    s_all = jnp.sum(sq, axis=-1, keepdims=True)
    r_lo = lax.rsqrt(s_lo * (1.0 / NA_HEADDIM) + EPS)
    r_hi = lax.rsqrt((s_all - s_lo) * (1.0 / NA_HEADDIM) + EPS)
    return x * jnp.where(lo_half, r_lo, r_hi) * gain


def _na_kernel(q_ref, kf_ref, kn_ref, vf_ref, vn_ref, km_ref, vm_ref, tb_ref, mb_ref, out_ref,
               k_scr, v_scr, s_scr, p_scr, l_scr, *, grid_rows):
    j = pl.program_id(2)
    lo_half = lax.broadcasted_iota(jnp.int32, (1, LANES), 1) < NA_HEADDIM
    step_lanes = NA_PAIRS_PER_STEP * LANES

    @pl.when(j == 0)
    def _():
        for scr, first_ref in ((k_scr, kf_ref), (v_scr, vf_ref)):
            scr[0:NA_BLOCK, :] = jnp.zeros((NA_BLOCK, step_lanes), BF16)
            scr[NA_BLOCK:2 * NA_BLOCK, :] = first_ref[...]

    @pl.when(j > 0)
    def _():
        for scr in (k_scr, v_scr):
            scr[0:NA_BLOCK, :] = scr[NA_BLOCK:2 * NA_BLOCK, :]
            scr[NA_BLOCK:2 * NA_BLOCK, :] = scr[2 * NA_BLOCK:3 * NA_BLOCK, :]

    k_scr[2 * NA_BLOCK:, :] = kn_ref[...]
    v_scr[2 * NA_BLOCK:, :] = vn_ref[...]

    nt = (((1,), (1,)), ((), ()))
    n_keys = WIN_H * GRID_W
    units = [(a, pp) for a in range(NA_ROWS_PER_BLOCK) for pp in range(NA_PAIRS_PER_STEP)]
    offs, d0s = [], []
    for a in range(NA_ROWS_PER_BLOCK):
        r = NA_ROWS_PER_BLOCK * j + a
        rstart = jnp.clip(r - WIN_H // 2, 0, grid_rows - WIN_H)
        offs.append(pl.multiple_of((rstart - NA_ROWS_PER_BLOCK * (j - 1)) * GRID_W, GRID_W))
        d0s.append(rstart - r + (WIN_H - 1))
    for u, (a, pp) in enumerate(units):
        lanes = slice(pp * LANES, (pp + 1) * LANES)
        q_r = q_ref[a * GRID_W:(a + 1) * GRID_W, lanes]
        zero = jnp.zeros_like(q_r)
        q2 = jnp.concatenate([jnp.where(lo_half, q_r, zero), jnp.where(lo_half, zero, q_r)], axis=0)
        s_scr[u, :, 0:n_keys] = (
            lax.dot_general(q2, k_scr[pl.ds(offs[a], n_keys), lanes], nt, preferred_element_type=F32)
            + tb_ref[pp, d0s[a]])
        s_scr[u, :, n_keys:] = (lax.dot_general(q2, km_ref[:, lanes], nt, preferred_element_type=F32)
                                + mb_ref[pp])
    for u in range(len(units)):
        s = s_scr[u]
        p = jnp.exp(s - jnp.max(s, axis=-1, keepdims=True))
        l_scr[u] = jnp.sum(p, axis=-1, keepdims=True)
        p_scr[u] = p.astype(BF16)
    for u, (a, pp) in enumerate(units):
        lanes = slice(pp * LANES, (pp + 1) * LANES)
        o = (jnp.dot(p_scr[u, :, 0:n_keys], v_scr[pl.ds(offs[a], n_keys), lanes], preferred_element_type=F32)
             + jnp.dot(p_scr[u, :, n_keys:], vm_ref[:, lanes], preferred_element_type=F32)) / l_scr[u]
        out_ref[a * GRID_W:(a + 1) * GRID_W, lanes] = (
            jnp.where(lo_half, o[0:GRID_W], o[GRID_W:]).astype(BF16))


def _na_meta_kernel(q_ref, k_ref, v_ref, mb_ref, out_ref):
    lo_half = lax.broadcasted_iota(jnp.int32, (1, LANES), 1) < NA_HEADDIM
    k_meta = k_ref[META_PAD:, :]
    v_meta = v_ref[META_PAD:, :]
    q = q_ref[...]
    nt = (((1,), (1,)), ((), ()))
    outs = []
    for hh in range(2):
        q_h = jnp.where(lo_half == (hh == 0), q, jnp.zeros_like(q))
        s = lax.dot_general(q_h, k_meta, nt, preferred_element_type=F32) + mb_ref[0, hh:hh + 1, 0:N_META]
        p = jnp.exp(s - jnp.max(s, axis=-1, keepdims=True))
        o = jnp.dot(p.astype(BF16), v_meta, preferred_element_type=F32) / jnp.sum(p, axis=-1, keepdims=True)
        outs.append(o)
    out_ref[...] = jnp.where(lo_half, outs[0], outs[1]).astype(BF16)


def _na(qk_tok, proj_tok, qk_meta, proj_meta, tb, mb2, bsz, T):
    nb = T // NA_BLOCK
    step_lanes = NA_PAIRS_PER_STEP * LANES
    qc, kc, vc = QK_COL_Q // step_lanes, QK_COL_K // step_lanes, COL_V // step_lanes
    blk = (NA_BLOCK, step_lanes)
    q_blk = lambda p, b, j: (b * nb + j, qc + p)
    first_blk = lambda col: (lambda p, b, j: (b * nb, col + p))
    next_blk = lambda col: (lambda p, b, j: (b * nb + jnp.minimum(j + 1, nb - 1), col + p))
    two_heads = 2 * GRID_W
    n_keys = WIN_H * GRID_W + CHUNK
    units = NA_ROWS_PER_BLOCK * NA_PAIRS_PER_STEP
    return pl.pallas_call(
        functools.partial(_na_kernel, grid_rows=T // GRID_W),
        grid=(HEAD_PAIRS_NA // NA_PAIRS_PER_STEP, bsz, nb),
        in_specs=[
            pl.BlockSpec(blk, q_blk),
            pl.BlockSpec(blk, first_blk(kc)),
            pl.BlockSpec(blk, next_blk(kc)),
            pl.BlockSpec(blk, first_blk(vc)),
            pl.BlockSpec(blk, next_blk(vc)),
            pl.BlockSpec((CHUNK, step_lanes), lambda p, b, j: (0, kc + p)),
            pl.BlockSpec((CHUNK, step_lanes), lambda p, b, j: (0, vc + p)),
            pl.BlockSpec((NA_PAIRS_PER_STEP, WIN_H, two_heads, WIN_H * GRID_W), lambda p, b, j: (p, 0, 0, 0)),
            pl.BlockSpec((NA_PAIRS_PER_STEP, two_heads, LANES), lambda p, b, j: (p, 0, 0)),
        ],
        out_specs=pl.BlockSpec(blk, lambda p, b, j: (b * nb + j, p)),
        out_shape=jax.ShapeDtypeStruct((bsz * T, D_NA), BF16),
        scratch_shapes=[
            pltpu.VMEM((3 * NA_BLOCK, step_lanes), BF16),
            pltpu.VMEM((3 * NA_BLOCK, step_lanes), BF16),
            pltpu.VMEM((units, two_heads, n_keys), F32),
            pltpu.VMEM((units, two_heads, n_keys), BF16),
            pltpu.VMEM((units, two_heads, 1), F32),
        ],
        compiler_params=_cparams(("arbitrary", "arbitrary", "arbitrary")),
        name="natten",
    )(qk_tok, qk_tok, qk_tok, proj_tok, proj_tok, qk_meta, proj_meta, tb, mb2)


def _na_meta(qk_meta, proj_meta, mb):
    qc, kc, vc = QK_COL_Q // LANES, QK_COL_K // LANES, COL_V // LANES
    blk = (CHUNK, LANES)
    return pl.pallas_call(
        _na_meta_kernel,
        grid=(HEAD_PAIRS_NA,),
        in_specs=[
            pl.BlockSpec(blk, lambda p: (0, qc + p)),
            pl.BlockSpec(blk, lambda p: (0, kc + p)),
            pl.BlockSpec(blk, lambda p: (0, vc + p)),
            pl.BlockSpec((1, 8, LANES), lambda p: (p, 0, 0)),
        ],
        out_specs=pl.BlockSpec(blk, lambda p: (0, p)),
        out_shape=jax.ShapeDtypeStruct((CHUNK, D_NA), BF16),
        compiler_params=_cparams(("arbitrary",)),
        name="natten_meta",
    )(qk_meta, qk_meta, proj_meta, mb)


def _rel_bias_table(rel_bias):
    w = np.arange(GRID_W)
    cstart = np.clip(w - WIN_W // 2, 0, GRID_W - WIN_W)
    cc = np.arange(GRID_W)
    inside = (cc[None, :] >= cstart[:, None]) & (cc[None, :] < cstart[:, None] + WIN_W)
    t = np.arange(2 * WIN_W - 1)
    onehot = ((cc[None, None, :] - w[None, :, None] + (WIN_W - 1)) == t[:, None, None]) & inside[None]
    rows = jnp.stack([rel_bias[:, d0:d0 + WIN_H] for d0 in range(WIN_H)], axis=1)
    rows = rows.reshape(HEAD_PAIRS_NA, 2, WIN_H, WIN_H, 2 * WIN_W - 1)
    g = jnp.einsum("phdit,twc->pdhwic", rows, jnp.asarray(onehot, F32), precision=lax.Precision.HIGHEST)
    g = jnp.where(inside[None, None, None, :, None, :], g, NEG_BIG)
    return g.reshape(HEAD_PAIRS_NA, WIN_H, 2 * GRID_W, WIN_H * GRID_W)


def _merge_kernel(ys_ref, yn_ref, gs_ref, gn_ref, h_ref, wbs_ref, wbn_ref, wo_ref, out_ref):
    m_s = jnp.dot(ys_ref[...], wbs_ref[...], preferred_element_type=F32)
    m_n = jnp.dot(yn_ref[...], wbn_ref[...], preferred_element_type=F32)
    merged = _sigmoid(gs_ref[...].astype(F32)) * m_s + _sigmoid(gn_ref[...].astype(F32)) * m_n
    out_ref[...] = h_ref[...] + jnp.dot(merged.astype(BF16), wo_ref[...], preferred_element_type=F32)


def _merge(y_ssd, y_na, proj, h, wbs, wbn, wo, tm, shared_rows):
    rows = y_ssd.shape[0]
    gsc, gnc = COL_GS // D_MODEL, COL_GN // D_MODEL
    r = (lambda i: 0) if shared_rows else (lambda i: i)
    full = lambda i: (0, 0)
    return pl.pallas_call(
        _merge_kernel,
        grid=(rows // tm,),
        in_specs=[
            pl.BlockSpec((tm, D_INNER), lambda i: (i, 0)),
            pl.BlockSpec((tm, D_NA), lambda i: (r(i), 0)),
            pl.BlockSpec((tm, D_MODEL), lambda i: (r(i), gsc)),
            pl.BlockSpec((tm, D_MODEL), lambda i: (r(i), gnc)),
            pl.BlockSpec((tm, D_MODEL), lambda i: (r(i), 0)),
            pl.BlockSpec((D_INNER, D_MODEL), full),
            pl.BlockSpec((D_NA, D_MODEL), full),
            pl.BlockSpec((D_MODEL, D_MODEL), full),
        ],
        out_specs=pl.BlockSpec((tm, D_MODEL), lambda i: (i, 0)),
        out_shape=jax.ShapeDtypeStruct((rows, D_MODEL), F32),
        compiler_params=_cparams(("parallel",)),
        name="merge",
    )(y_ssd, y_na, proj, proj, h, wbs, wbn, wo)


FFN_HALO = 16
FFN_TF = 256


def _ffn_kernel(h_ref, prev_ref, next_ref, meta_ref, g_ref, wup_ref, cw_ref, cb_ref, wd_ref, out_ref,
                u_scr, act_scr, *, tiles_per_seq, tm):
    pos = pl.program_id(0) % tiles_per_seq
    gain = g_ref[...]
    prev = jnp.where(pos == 0, meta_ref[...], prev_ref[...])
    nxt = jnp.where(pos == tiles_per_seq - 1, 0.0, next_ref[...])
    u_scr[0:FFN_HALO, :] = _rmsnorm_rows(prev, gain).astype(BF16)
    u_scr[FFN_HALO:FFN_HALO + tm, :] = _rmsnorm_rows(h_ref[...], gain).astype(BF16)
    u_scr[FFN_HALO + tm:, :] = _rmsnorm_rows(nxt, gain).astype(BF16)
    u = u_scr[...]

    n_ext = tm + 2 * FFN_HALO

    def conv(x_ext, cols):
        y = cw_ref[1:2, cols] * x_ext + cb_ref[:, cols]
        y = y + pltpu.roll(cw_ref[0:1, cols] * x_ext, 1, 0)
        y = y + pltpu.roll(cw_ref[2:3, cols] * x_ext, n_ext - 1, 0)
        return y[FFN_HALO:FFN_HALO + tm]

    for k in range(D_FF // FFN_TF):
        cols_a = slice(k * FFN_TF, (k + 1) * FFN_TF)
        cols_g = slice(D_FF + k * FFN_TF, D_FF + (k + 1) * FFN_TF)
        a = conv(jnp.dot(u, wup_ref[:, cols_a], preferred_element_type=F32), cols_a)
        g = conv(jnp.dot(u, wup_ref[:, cols_g], preferred_element_type=F32), cols_g)
        act_scr[:, cols_a] = (g * _sigmoid(g) * a).astype(BF16)
    out_ref[...] = h_ref[...] + jnp.dot(act_scr[...], wd_ref[...], preferred_element_type=F32)


def _ffn(h1, h1_meta, g_ffn, w_up, cw, cb, w_down, bsz, T, tm):
    rows = bsz * T
    tps = T // tm
    hb = tm // FFN_HALO
    mb = CHUNK // FFN_HALO
    nblk = rows // FFN_HALO
    const = lambda i: (0, 0)
    resident = pl.Buffered(1)
    return pl.pallas_call(
        functools.partial(_ffn_kernel, tiles_per_seq=tps, tm=tm),
        grid=(rows // tm,),
        in_specs=[
            pl.BlockSpec((tm, D_MODEL), lambda i: (i, 0)),
            pl.BlockSpec((FFN_HALO, D_MODEL), lambda i: (jnp.maximum(i * hb - 1, 0), 0)),
            pl.BlockSpec((FFN_HALO, D_MODEL), lambda i: (jnp.minimum((i + 1) * hb, nblk - 1), 0)),
            pl.BlockSpec((FFN_HALO, D_MODEL), lambda i: ((i // tps) * mb + mb - 1, 0)),
            pl.BlockSpec((1, D_MODEL), const),
            pl.BlockSpec((D_MODEL, 2 * D_FF), const, pipeline_mode=resident),
            pl.BlockSpec((8, 2 * D_FF), const),
            pl.BlockSpec((1, 2 * D_FF), const),
            pl.BlockSpec((D_FF, D_MODEL), const, pipeline_mode=resident),
        ],
        out_specs=pl.BlockSpec((tm, D_MODEL), lambda i: (i, 0)),
        out_shape=jax.ShapeDtypeStruct((rows, D_MODEL), F32),
        scratch_shapes=[
            pltpu.VMEM((tm + 2 * FFN_HALO, D_MODEL), BF16),
            pltpu.VMEM((tm, D_FF), BF16),
        ],
        compiler_params=_cparams(("parallel",)),
        name="ffn",
    )(h1, h1, h1, h1_meta, g_ffn, w_up, cw, cb, w_down)


def _pad_rows(a, n):
    return jnp.pad(a, ((0, n - a.shape[0]), (0, 0)))


def _row_tile(rows, cap):
    tm = min(cap, rows)
    assert rows % tm == 0
    return tm


def kernel(x_prompt, x_sample, meta_tokens, g_mix, w_in, ssd_conv_w, ssd_conv_b, dt_bias_f, dt_bias_b,
           a_log_f, a_log_b, d_skip, ssd_norm_g, q_norm_g, k_norm_g, rel_bias, meta_bias, w_br_ssd,
           w_br_na, w_out, g_ffn, w_up, ffn_conv_w, ffn_conv_b, w_down):
    assert g_mix.shape[0] == 1, "single-layer block"
    w = w_in[0]
    o_z, o_xbc = 0, D_INNER
    o_dtf = o_xbc + D_XBC
    o_q = o_dtf + 2 * SSD_HEADS
    o_v = o_q + 2 * D_NA
    w_main = jnp.concatenate([w[:, o_xbc:o_dtf], w[:, o_z:o_xbc], w[:, o_v:]], axis=1).astype(BF16)
    w_qk = w[:, o_q:o_v].astype(BF16)
    w_dt = jnp.pad(w[:, o_dtf:o_q], ((0, 0), (0, LANES - 2 * SSD_HEADS))).astype(BF16)
    g_mix2 = g_mix.astype(F32)
    lane_pad = lambda v: jnp.pad(v, (0, LANES - v.shape[0]))[None, :].astype(F32)
    dtb = lane_pad(jnp.concatenate([dt_bias_f[0], dt_bias_b[0]]))
    alog = lane_pad(jnp.concatenate([a_log_f[0], a_log_b[0]]))
    dskip = jnp.repeat(d_skip[0].astype(F32), SSD_HEADDIM)[None, :]
    ng = ssd_norm_g.astype(F32)
    conv_w = _pad_rows(ssd_conv_w[0].astype(F32), 8)
    conv_b = ssd_conv_b.astype(F32)
    qg = jnp.tile(q_norm_g[0].astype(F32), 2)[None, :] * (NA_HEADDIM ** -0.5)
    kg = jnp.tile(k_norm_g[0].astype(F32), 2)[None, :]
    qk_gain = jnp.stack([qg, kg])
    tb = _rel_bias_table(rel_bias[0].astype(F32))
    mb = jnp.pad(meta_bias[0].astype(F32).reshape(HEAD_PAIRS_NA, 2, N_META), ((0, 0), (0, 6), (0, LANES - N_META)))
    mb2 = jnp.repeat(meta_bias[0].astype(F32).reshape(HEAD_PAIRS_NA, 2, N_META), GRID_W, axis=1)
    mb2 = jnp.pad(mb2, ((0, 0), (0, 0), (META_PAD, 0)), constant_values=NEG_BIG)
    wbs, wbn, wo = w_br_ssd[0].astype(BF16), w_br_na[0].astype(BF16), w_out[0].astype(BF16)
    w_up_b, w_down_b = w_up[0].astype(BF16), w_down[0].astype(BF16)
    ffn_cw = _pad_rows(ffn_conv_w[0].astype(F32), 8)
    ffn_cb = ffn_conv_b.astype(F32)

    h_meta = jnp.concatenate([jnp.zeros((META_PAD, D_MODEL), F32), meta_tokens.astype(F32)], axis=0)
    proj_meta, dt_meta, qk_meta = _inproj(h_meta, g_mix2, w_main, w_dt, w_qk, qk_gain, CHUNK)
    dk_meta = _dt_prep(dt_meta, dtb, alog, 1, True)
    y_na_meta = _na_meta(qk_meta, proj_meta, mb)

    def run(x):
        bsz, T, _ = x.shape
        assert T % NA_BLOCK == 0 and T % CHUNK == 0
        x2d = x.reshape(bsz * T, D_MODEL)
        tm = _row_tile(bsz * T, 1024)
        proj_tok, dt_tok, qk_tok = _inproj(x2d, g_mix2, w_main, w_dt, w_qk, qk_gain, _row_tile(bsz * T, 2048))
        dk_tok = _dt_prep(dt_tok, dtb, alog, DT_PREP_CHUNKS, False)
        y_ssd, y_ssd_meta = _ssd(dk_tok, dk_meta, proj_tok, proj_meta, conv_w, conv_b, dskip, ng, bsz, T)
        y_na = _na(qk_tok, proj_tok, qk_meta, proj_meta, tb, mb2, bsz, T)
        h1 = _merge(y_ssd, y_na, proj_tok, x2d, wbs, wbn, wo, tm, False)
        h1_meta = _merge(y_ssd_meta, y_na_meta, proj_meta, h_meta, wbs, wbn, wo, CHUNK, True)
        tf = _row_tile(T, 1024)
        out = _ffn(h1, h1_meta, g_ffn.astype(F32), w_up_b, ffn_cw, ffn_cb, w_down_b, bsz, T, tf)
        return out.reshape(bsz, T, D_MODEL)

    return (run(x_prompt), run(x_sample))
```

```python
import functools

import numpy as np
import jax
import jax.numpy as jnp
from jax import lax
from jax.experimental import pallas as pl
from jax.experimental.pallas import tpu as pltpu

F32 = jnp.float32
BF16 = jnp.bfloat16

D_MODEL = 1024
N_META = 16
GRID_W = 64
D_INNER = 2048
SSD_HEADDIM = 64
SSD_HEADS = 32
SSD_GROUPS = 8
D_STATE = 128
D_CONV = 5
CHUNK = 128
D_XBC = D_INNER + 2 * SSD_GROUPS * D_STATE
NA_HEADS = 16
NA_HEADDIM = 64
D_NA = NA_HEADS * NA_HEADDIM
WIN_H = 8
WIN_W = 16
D_FF = 2816
FFN_CONV = 3
EPS = 1e-6

LANES = 128
META_PAD = CHUNK - N_META
HEAD_PAIRS_SSD = SSD_HEADS // 2
HEAD_PAIRS_NA = NA_HEADS // 2
NA_ROWS_PER_BLOCK = 16
NA_BLOCK = NA_ROWS_PER_BLOCK * GRID_W
NA_PAIRS_PER_STEP = 2
NEG_BIG = -1e30

COL_XBC = 0
COL_Z = COL_XBC + D_XBC
COL_V = COL_Z + D_INNER
COL_GS = COL_V + D_NA
COL_GN = COL_GS + D_MODEL
D_PROJ = COL_GN + D_MODEL
QK_COL_Q = 0
QK_COL_K = D_NA

VMEM_LIMIT = 56 * 1024 * 1024


def _cparams(sem):
    return pltpu.CompilerParams(dimension_semantics=sem, vmem_limit_bytes=VMEM_LIMIT)


def _sigmoid(x):
    return 1.0 / (1.0 + jnp.exp(-x))


def _softplus(x):
    return jnp.maximum(x, 0.0) + jnp.log(1.0 + jnp.exp(-jnp.abs(x)))


def _rmsnorm_rows(x, g):
    ms = jnp.mean(x * x, axis=-1, keepdims=True)
    return x * lax.rsqrt(ms + EPS) * g


INPROJ_TN = 1024


def _inproj_kernel(x_ref, g_ref, w_ref, wdt_ref, out_ref, dt_ref, u_scr):
    @pl.when(pl.program_id(1) == 0)
    def _():
        u = _rmsnorm_rows(x_ref[...], g_ref[...]).astype(BF16)
        u_scr[...] = u
        dt_ref[...] = jnp.dot(u, wdt_ref[...], preferred_element_type=F32)

    out_ref[...] = jnp.dot(u_scr[...], w_ref[...], preferred_element_type=F32).astype(BF16)


def _inproj_qk_kernel(x_ref, g_ref, w_ref, gain_ref, out_ref, u_scr):
    @pl.when(pl.program_id(1) == 0)
    def _():
        u_scr[...] = _rmsnorm_rows(x_ref[...], g_ref[...]).astype(BF16)

    res = jnp.dot(u_scr[...], w_ref[...], preferred_element_type=F32)
    lo_half = lax.broadcasted_iota(jnp.int32, (1, LANES), 1) < NA_HEADDIM
    gain = gain_ref[0]
    for t in range(HEAD_PAIRS_NA):
        cols = slice(t * LANES, (t + 1) * LANES)
        out_ref[:, cols] = _head_rmsnorm(res[:, cols], gain, lo_half).astype(BF16)


def _inproj(x2d, g_mix, w_main, w_dt, w_qk, qk_gain, tm):
    rows = x2d.shape[0]
    tn = INPROJ_TN
    row_blk = pl.BlockSpec((tm, D_MODEL), lambda i, j: (i, 0))
    gain_blk = pl.BlockSpec((1, D_MODEL), lambda i, j: (0, 0))
    proj, dt = pl.pallas_call(
        _inproj_kernel,
        grid=(rows // tm, D_PROJ // tn),
        in_specs=[
            row_blk,
            gain_blk,
            pl.BlockSpec((D_MODEL, tn), lambda i, j: (0, j)),
            pl.BlockSpec((D_MODEL, LANES), lambda i, j: (0, 0)),
        ],
        out_specs=[
            pl.BlockSpec((tm, tn), lambda i, j: (i, j)),
            pl.BlockSpec((tm, LANES), lambda i, j: (i, 0)),
        ],
        out_shape=[
            jax.ShapeDtypeStruct((rows, D_PROJ), BF16),
            jax.ShapeDtypeStruct((rows, LANES), F32),
        ],
        scratch_shapes=[pltpu.VMEM((tm, D_MODEL), BF16)],
        compiler_params=_cparams(("parallel", "arbitrary")),
        name="inproj",
    )(x2d, g_mix, w_main, w_dt)
    qk = pl.pallas_call(
        _inproj_qk_kernel,
        grid=(rows // tm, 2),
        in_specs=[
            row_blk,
            gain_blk,
            pl.BlockSpec((D_MODEL, D_NA), lambda i, j: (0, j)),
            pl.BlockSpec((1, 1, LANES), lambda i, j: (j, 0, 0)),
        ],
        out_specs=pl.BlockSpec((tm, D_NA), lambda i, j: (i, j)),
        out_shape=jax.ShapeDtypeStruct((rows, 2 * D_NA), BF16),
        scratch_shapes=[pltpu.VMEM((tm, D_MODEL), BF16)],
        compiler_params=_cparams(("parallel", "arbitrary")),
        name="inproj_qk",
    )(x2d, g_mix, w_qk, qk_gain)
    return proj, dt, qk


HALO = 16


CONV_PAD = (D_CONV - 1) // 2
CONV_SIDE_TAPS = tuple(k for k in range(D_CONV) if k != CONV_PAD)


def _conv_shift_matrix():
    sh = np.zeros((len(CONV_SIDE_TAPS) * CHUNK, CHUNK + 2 * HALO), np.float32)
    r = np.arange(CHUNK)
    for j, k in enumerate(CONV_SIDE_TAPS):
        sh[j * CHUNK + r, HALO + r + k - CONV_PAD] = 1.0
    return jnp.asarray(sh, BF16)


def _conv_silu_chunk(c, nc, tok_ref, prev_ref, next_ref, meta_ref, shift_ref, w_ref, b_ref, out_ref):
    is_meta = c == 0
    strip = 2 * LANES
    for s in range(D_XBC // strip):
        cols = slice(s * strip, (s + 1) * strip)
        main = jnp.where(is_meta, meta_ref[:, cols], tok_ref[:, cols])
        prev = jnp.where(c == 1, meta_ref[CHUNK - HALO:, cols], prev_ref[:, cols])
        prev = jnp.where(is_meta, jnp.zeros_like(prev), prev)
        nxt = jnp.where(c == nc - 1, jnp.zeros_like(prev), next_ref[:, cols])
        ext = jnp.concatenate([prev, main, nxt], axis=0)
        acc = w_ref[CONV_PAD:CONV_PAD + 1, cols] * main.astype(F32) + b_ref[:, cols]
        for j, k in enumerate(CONV_SIDE_TAPS):
            shifted = jnp.dot(shift_ref[j * CHUNK:(j + 1) * CHUNK, :], ext, preferred_element_type=F32)
            acc = acc + w_ref[k:k + 1, cols] * shifted
        out_ref[:, cols] = (acc * _sigmoid(acc)).astype(BF16)


DK_DT, DK_CS, DK_ECS, DK_CS_T, DK_ECS_T, DK_DT_T = range(6)
DK_N = 6
DT_PREP_CHUNKS = 16


def _dt_prep_kernel(dt_ref, dtb_ref, alog_ref, out_ref, *, n_chunks, zero_pad_rows):
    row = lax.broadcasted_iota(jnp.int32, (CHUNK, LANES), 0)
    col = lax.broadcasted_iota(jnp.int32, (CHUNK, LANES), 1)
    tri = (col <= row).astype(F32)
    neg_a = -jnp.exp(alog_ref[...])
    for i in range(n_chunks):
        dt = _softplus(dt_ref[i * CHUNK:(i + 1) * CHUNK, :] + dtb_ref[...])
        if zero_pad_rows:
            dt = jnp.where(row < META_PAD, 0.0, dt)
        a = dt * neg_a
        cs = jnp.dot(tri, a, preferred_element_type=F32, precision=lax.Precision.HIGHEST)
        ecs = cs - a
        out_ref[i, DK_DT] = dt
        out_ref[i, DK_CS] = cs
        out_ref[i, DK_ECS] = ecs
        out_ref[i, DK_CS_T] = cs.T
        out_ref[i, DK_ECS_T] = ecs.T
        out_ref[i, DK_DT_T] = dt.T


def _dt_prep(dt_raw, dtb, alog, chunks_per_step, zero_pad_rows):
    n = dt_raw.shape[0] // CHUNK
    assert n % chunks_per_step == 0
    small = lambda i: (0, 0)
    return pl.pallas_call(
        functools.partial(_dt_prep_kernel, n_chunks=chunks_per_step, zero_pad_rows=zero_pad_rows),
        grid=(n // chunks_per_step,),
        in_specs=[
            pl.BlockSpec((chunks_per_step * CHUNK, LANES), lambda i: (i, 0)),
            pl.BlockSpec((1, LANES), small),
            pl.BlockSpec((1, LANES), small),
        ],
        out_specs=pl.BlockSpec((chunks_per_step, DK_N, CHUNK, LANES), lambda i: (i, 0, 0, 0)),
        out_shape=jax.ShapeDtypeStruct((n, DK_N, CHUNK, LANES), F32),
        compiler_params=_cparams(("parallel",)),
        name="dt_prep",
    )(dt_raw, dtb, alog)


def _lane_pair(col_lo, col_hi, lo_mask):
    return jnp.where(lo_mask, col_lo, col_hi)


def _decay_tiles(c, dk_tok_ref, dk_meta_ref):
    return lambda k: jnp.where(c == 0, dk_meta_ref[0, k], dk_tok_ref[0, k])


def _ssd_fwd_kernel(tok_ref, prev_ref, next_ref, meta_ref, shift_ref, w_ref, b_ref, dk_tok_ref, dk_meta_ref,
                    xbc_ref, y_ref, state_scr, *, nc):
    c = pl.program_id(1)

    @pl.when(c == 0)
    def _():
        state_scr[...] = jnp.zeros_like(state_scr)

    _conv_silu_chunk(c, nc, tok_ref, prev_ref, next_ref, meta_ref, shift_ref, w_ref, b_ref, xbc_ref)

    dk = _decay_tiles(c, dk_tok_ref, dk_meta_ref)
    dt, cs, ecs, cs_t, ecs_t, dt_t = (dk(k) for k in (DK_DT, DK_CS, DK_ECS, DK_CS_T, DK_ECS_T, DK_DT_T))
    row = lax.broadcasted_iota(jnp.int32, (CHUNK, LANES), 0)
    col = lax.broadcasted_iota(jnp.int32, (CHUNK, LANES), 1)
    lower = row >= col
    strict_lower = row > col
    strict_upper = row < col
    lo_half = col < SSD_HEADDIM
    tot = cs[CHUNK - 1:CHUNK, :]
    heads_per_group = SSD_HEADS // SSD_GROUPS
    gw = heads_per_group * SSD_HEADDIM
    head_of_lane = lax.broadcasted_iota(jnp.int32, (CHUNK, gw), 1) // SSD_HEADDIM

    for g in range(SSD_GROUPS):
        b_g = xbc_ref[:, D_INNER + g * D_STATE:D_INNER + (g + 1) * D_STATE]
        c_g = xbc_ref[:, D_INNER + SSD_GROUPS * D_STATE + g * D_STATE:
                      D_INNER + SSD_GROUPS * D_STATE + (g + 1) * D_STATE]
        cb = lax.dot_general(c_g, b_g, (((1,), (1,)), ((), ())), preferred_element_type=F32)
        st = state_scr[g]
        y_off = jnp.dot(c_g, st.astype(BF16), preferred_element_type=F32)
        x_g = xbc_ref[:, g * gw:(g + 1) * gw]
        m_mats, x_blocks, e_cols, w_cols, decs = [], [], [], [], []
        for r in range(heads_per_group):
            h = g * heads_per_group + r
            hb = SSD_HEADS + h
            csf_col = cs[:, h:h + 1]
            e_mat = jnp.where(lower, csf_col - cs_t[h:h + 1, :], ecs_t[hb:hb + 1, :] - ecs[:, hb:hb + 1])
            dtf_row, dtb_row = dt_t[h:h + 1, :], dt_t[hb:hb + 1, :]
            dsel = jnp.where(strict_lower, dtf_row, jnp.where(strict_upper, dtb_row, dtf_row + dtb_row))
            m_mats.append((cb * jnp.exp(e_mat) * dsel).astype(BF16))
            x_blocks.append(jnp.where(head_of_lane == r, x_g, jnp.zeros_like(x_g)))
            e_cols.append(jnp.exp(csf_col))
            w_cols.append(dt[:, h:h + 1] * jnp.exp(tot[:, h:h + 1] - csf_col))
            decs.append(jnp.exp(tot[:, h:h + 1]))
        y_diag = jnp.dot(jnp.concatenate(m_mats, axis=1), jnp.concatenate(x_blocks, axis=0),
                         preferred_element_type=F32)
        xw_tiles, dec_tiles = [], []
        for pr in range(2):
            tile = g * 2 + pr
            sl = slice(pr * LANES, (pr + 1) * LANES)
            e_pair = _lane_pair(e_cols[2 * pr], e_cols[2 * pr + 1], lo_half)
            y_ref[:, tile * LANES:(tile + 1) * LANES] = y_diag[:, sl] + e_pair * y_off[:, sl]
            w_pair = _lane_pair(w_cols[2 * pr], w_cols[2 * pr + 1], lo_half)
            xw_tiles.append((x_g[:, sl].astype(F32) * w_pair).astype(BF16))
            dec_tiles.append(_lane_pair(decs[2 * pr], decs[2 * pr + 1], lo_half[0:1, :]))
        xw = jnp.concatenate(xw_tiles, axis=1)
        dec = jnp.concatenate(dec_tiles, axis=1)
        upd = lax.dot_general(b_g, xw, (((0,), (0,)), ((), ())), preferred_element_type=F32)
        state_scr[g] = st * dec + upd


def _ssd_bwd_kernel(xbc_ref, dk_tok_ref, dk_meta_ref, z_tok_ref, z_meta_ref, yf_ref,
                    dskip_ref, ng_ref, y_tok_ref, y_meta_ref, state_scr, y_scr, *, nc):
    i = pl.program_id(1)
    c = nc - 1 - i

    @pl.when(i == 0)
    def _():
        state_scr[...] = jnp.zeros_like(state_scr)

    dk = _decay_tiles(c, dk_tok_ref, dk_meta_ref)
    dt, ecs = dk(DK_DT), dk(DK_ECS)
    lo_half = lax.broadcasted_iota(jnp.int32, (CHUNK, LANES), 1) < SSD_HEADDIM
    tot = dk(DK_CS)[CHUNK - 1:CHUNK, :]

    for g in range(SSD_GROUPS):
        b_g = xbc_ref[:, D_INNER + g * D_STATE:D_INNER + (g + 1) * D_STATE]
        c_g = xbc_ref[:, D_INNER + SSD_GROUPS * D_STATE + g * D_STATE:
                      D_INNER + SSD_GROUPS * D_STATE + (g + 1) * D_STATE]
        st = state_scr[g]
        y_off = jnp.dot(c_g, st.astype(BF16), preferred_element_type=F32)
        xw_tiles, dec_tiles = [], []
        for pr in range(2):
            tile = g * 2 + pr
            x_pair = xbc_ref[:, tile * LANES:(tile + 1) * LANES].astype(F32)
            e_cols, w_cols, decs = [], [], []
            for hh in range(2):
                hb = SSD_HEADS + g * 4 + pr * 2 + hh
                ecs_col = ecs[:, hb:hb + 1]
                e_cols.append(jnp.exp(tot[:, hb:hb + 1] - ecs_col))
                w_cols.append(dt[:, hb:hb + 1] * jnp.exp(ecs_col))
                decs.append(jnp.exp(tot[:, hb:hb + 1]))
            sl = slice(tile * LANES, (tile + 1) * LANES)
            y_pair = (yf_ref[:, sl] + _lane_pair(e_cols[0], e_cols[1], lo_half) * y_off[:, pr * LANES:(pr + 1) * LANES]
                      + dskip_ref[:, sl] * x_pair)
            z_pair = jnp.where(c == 0, z_meta_ref[:, sl], z_tok_ref[:, sl]).astype(F32)
            y_scr[:, sl] = y_pair * (z_pair * _sigmoid(z_pair))
            xw_tiles.append((x_pair * _lane_pair(w_cols[0], w_cols[1], lo_half)).astype(BF16))
            dec_tiles.append(_lane_pair(decs[0], decs[1], lo_half[0:1, :]))
        xw = jnp.concatenate(xw_tiles, axis=1)
        dec = jnp.concatenate(dec_tiles, axis=1)
        upd = lax.dot_general(b_g, xw, (((0,), (0,)), ((), ())), preferred_element_type=F32)
        state_scr[g] = st * dec + upd

    gw = D_INNER // SSD_GROUPS
    for g in range(SSD_GROUPS):
        yg = y_scr[:, g * gw:(g + 1) * gw]
        yg = yg * lax.rsqrt(jnp.mean(yg * yg, axis=-1, keepdims=True) + EPS) * ng_ref[:, g * gw:(g + 1) * gw]
        y_scr[:, g * gw:(g + 1) * gw] = yg

    @pl.when(c > 0)
    def _():
        y_tok_ref[...] = y_scr[...].astype(BF16)

    @pl.when(c == 0)
    def _():
        y_meta_ref[...] = y_scr[...].astype(BF16)


def _ssd(dk_tok, dk_meta, proj_tok, proj_meta, conv_w, conv_b, dskip, ng, bsz, T):
    nct = T // CHUNK
    nc = nct + 1
    hb = CHUNK // HALO
    small = lambda b, c: (0, 0)
    dk_blk = (1, DK_N, CHUNK, LANES)
    dk_first = lambda b, c: (0, 0, 0, 0)
    state = pltpu.VMEM((SSD_GROUPS, D_STATE, 4 * SSD_HEADDIM), F32)
    shift = _conv_shift_matrix()
    xbc, y_f = pl.pallas_call(
        functools.partial(_ssd_fwd_kernel, nc=nc),
        grid=(bsz, nc),
        in_specs=[
            pl.BlockSpec((CHUNK, D_XBC), lambda b, c: (b * nct + jnp.maximum(c - 1, 0), 0)),
            pl.BlockSpec((HALO, D_XBC), lambda b, c: (jnp.maximum(b * nct * hb + (c - 1) * hb - 1, 0), 0)),
            pl.BlockSpec((HALO, D_XBC), lambda b, c: (b * nct * hb + jnp.minimum(c, nct - 1) * hb, 0)),
            pl.BlockSpec((CHUNK, D_XBC), small),
            pl.BlockSpec(shift.shape, small),
            pl.BlockSpec((8, D_XBC), small),
            pl.BlockSpec((1, D_XBC), small),
            pl.BlockSpec(dk_blk, lambda b, c: (b * nct + jnp.maximum(c - 1, 0), 0, 0, 0)),
            pl.BlockSpec(dk_blk, dk_first),
        ],
        out_specs=[
            pl.BlockSpec((CHUNK, D_XBC), lambda b, c: (b * nc + c, 0)),
            pl.BlockSpec((CHUNK, D_INNER), lambda b, c: (b * nc + c, 0)),
        ],
        out_shape=[
            jax.ShapeDtypeStruct((bsz * nc * CHUNK, D_XBC), BF16),
            jax.ShapeDtypeStruct((bsz * nc * CHUNK, D_INNER), F32),
        ],
        scratch_shapes=[state],
        compiler_params=_cparams(("parallel", "arbitrary")),
        name="ssd_fwd",
    )(proj_tok, proj_tok, proj_tok, proj_meta, shift, conv_w, conv_b, dk_tok, dk_meta)

    zb = COL_Z // D_INNER
    rev = lambda b, i: nc - 1 - i
    tok_blk = lambda b, i: b * nct + jnp.maximum(rev(b, i) - 1, 0)
    y_tok, y_meta = pl.pallas_call(
        functools.partial(_ssd_bwd_kernel, nc=nc),
        grid=(bsz, nc),
        in_specs=[
            pl.BlockSpec((CHUNK, D_XBC), lambda b, i: (b * nc + rev(b, i), 0)),
            pl.BlockSpec(dk_blk, lambda b, i: (tok_blk(b, i), 0, 0, 0)),
            pl.BlockSpec(dk_blk, dk_first),
            pl.BlockSpec((CHUNK, D_INNER), lambda b, i: (tok_blk(b, i), zb)),
            pl.BlockSpec((CHUNK, D_INNER), lambda b, i: (0, zb)),
            pl.BlockSpec((CHUNK, D_INNER), lambda b, i: (b * nc + rev(b, i), 0)),
            pl.BlockSpec((1, D_INNER), small),
            pl.BlockSpec((1, D_INNER), small),
        ],
        out_specs=[
            pl.BlockSpec((CHUNK, D_INNER), lambda b, i: (tok_blk(b, i), 0)),
            pl.BlockSpec((CHUNK, D_INNER), lambda b, i: (b, 0)),
        ],
        out_shape=[
            jax.ShapeDtypeStruct((bsz * T, D_INNER), BF16),
            jax.ShapeDtypeStruct((bsz * CHUNK, D_INNER), BF16),
        ],
        scratch_shapes=[state, pltpu.VMEM((CHUNK, D_INNER), F32)],
        compiler_params=_cparams(("parallel", "arbitrary")),
        name="ssd_bwd",
    )(xbc, dk_tok, dk_meta, proj_tok, proj_meta, y_f, dskip, ng)
    return y_tok, y_meta


def _head_rmsnorm(x, gain, lo_half):
    sq = x * x
    s_lo = jnp.sum(jnp.where(lo_half, sq, 0.0), axis=-1, keepdims=True)
    s_all = jnp.sum(sq, axis=-1, keepdims=True)
    r_lo = lax.rsqrt(s_lo * (1.0 / NA_HEADDIM) + EPS)
    r_hi = lax.rsqrt((s_all - s_lo) * (1.0 / NA_HEADDIM) + EPS)
    return x * jnp.where(lo_half, r_lo, r_hi) * gain


def _na_kernel(q_ref, kf_ref, kn_ref, vf_ref, vn_ref, km_ref, vm_ref, tb_ref, mb_ref, out_ref,
               k_scr, v_scr, s_scr, p_scr, l_scr, *, grid_rows):
    j = pl.program_id(2)
    lo_half = lax.broadcasted_iota(jnp.int32, (1, LANES), 1) < NA_HEADDIM
    step_lanes = NA_PAIRS_PER_STEP * LANES

    @pl.when(j == 0)
    def _():
        for scr, first_ref in ((k_scr, kf_ref), (v_scr, vf_ref)):
            scr[0:NA_BLOCK, :] = jnp.zeros((NA_BLOCK, step_lanes), BF16)
            scr[NA_BLOCK:2 * NA_BLOCK, :] = first_ref[...]

    @pl.when(j > 0)
    def _():
        for scr in (k_scr, v_scr):
            scr[0:NA_BLOCK, :] = scr[NA_BLOCK:2 * NA_BLOCK, :]
            scr[NA_BLOCK:2 * NA_BLOCK, :] = scr[2 * NA_BLOCK:3 * NA_BLOCK, :]

    k_scr[2 * NA_BLOCK:, :] = kn_ref[...]
    v_scr[2 * NA_BLOCK:, :] = vn_ref[...]

    nt = (((1,), (1,)), ((), ()))
    n_keys = WIN_H * GRID_W
    units = [(a, pp) for a in range(NA_ROWS_PER_BLOCK) for pp in range(NA_PAIRS_PER_STEP)]
    offs, d0s = [], []
    for a in range(NA_ROWS_PER_BLOCK):
        r = NA_ROWS_PER_BLOCK * j + a
        rstart = jnp.clip(r - WIN_H // 2, 0, grid_rows - WIN_H)
        offs.append(pl.multiple_of((rstart - NA_ROWS_PER_BLOCK * (j - 1)) * GRID_W, GRID_W))
        d0s.append(rstart - r + (WIN_H - 1))
    for u, (a, pp) in enumerate(units):
        lanes = slice(pp * LANES, (pp + 1) * LANES)
        q_r = q_ref[a * GRID_W:(a + 1) * GRID_W, lanes]
        zero = jnp.zeros_like(q_r)
        q2 = jnp.concatenate([jnp.where(lo_half, q_r, zero), jnp.where(lo_half, zero, q_r)], axis=0)
        bias = jnp.concatenate(
            [jnp.concatenate([tb_ref[pp, hh, d0s[a] + 2 * i2] for i2 in range(WIN_H // 2)], axis=1)
             for hh in range(2)], axis=0)
        s_scr[u, :, 0:n_keys] = (
            lax.dot_general(q2, k_scr[pl.ds(offs[a], n_keys), lanes], nt, preferred_element_type=F32) + bias)
        s_scr[u, :, n_keys:] = (lax.dot_general(q2, km_ref[:, lanes], nt, preferred_element_type=F32)
                                + mb_ref[pp])
    for u in range(len(units)):
        s = s_scr[u]
        p = jnp.exp(s - jnp.max(s, axis=-1, keepdims=True))
        l_scr[u] = jnp.sum(p, axis=-1, keepdims=True)
        p_scr[u] = p.astype(BF16)
    for u, (a, pp) in enumerate(units):
        lanes = slice(pp * LANES, (pp + 1) * LANES)
        o = (jnp.dot(p_scr[u, :, 0:n_keys], v_scr[pl.ds(offs[a], n_keys), lanes], preferred_element_type=F32)
             + jnp.dot(p_scr[u, :, n_keys:], vm_ref[:, lanes], preferred_element_type=F32)) / l_scr[u]
        out_ref[a * GRID_W:(a + 1) * GRID_W, lanes] = (
            jnp.where(lo_half, o[0:GRID_W], o[GRID_W:]).astype(BF16))


def _na_meta_kernel(q_ref, k_ref, v_ref, mb_ref, out_ref):
    lo_half = lax.broadcasted_iota(jnp.int32, (1, LANES), 1) < NA_HEADDIM
    k_meta = k_ref[META_PAD:, :]
    v_meta = v_ref[META_PAD:, :]
    q = q_ref[...]
    nt = (((1,), (1,)), ((), ()))
    outs = []
    for hh in range(2):
        q_h = jnp.where(lo_half == (hh == 0), q, jnp.zeros_like(q))
        s = lax.dot_general(q_h, k_meta, nt, preferred_element_type=F32) + mb_ref[0, hh:hh + 1, 0:N_META]
        p = jnp.exp(s - jnp.max(s, axis=-1, keepdims=True))
        o = jnp.dot(p.astype(BF16), v_meta, preferred_element_type=F32) / jnp.sum(p, axis=-1, keepdims=True)
        outs.append(o)
    out_ref[...] = jnp.where(lo_half, outs[0], outs[1]).astype(BF16)


def _na(qk_tok, proj_tok, qk_meta, proj_meta, tb, mb2, bsz, T):
    nb = T // NA_BLOCK
    step_lanes = NA_PAIRS_PER_STEP * LANES
    qc, kc, vc = QK_COL_Q // step_lanes, QK_COL_K // step_lanes, COL_V // step_lanes
    blk = (NA_BLOCK, step_lanes)
    q_blk = lambda p, b, j: (b * nb + j, qc + p)
    first_blk = lambda col: (lambda p, b, j: (b * nb, col + p))
    next_blk = lambda col: (lambda p, b, j: (b * nb + jnp.minimum(j + 1, nb - 1), col + p))
    two_heads = 2 * GRID_W
    n_keys = WIN_H * GRID_W + CHUNK
    units = NA_ROWS_PER_BLOCK * NA_PAIRS_PER_STEP
    return pl.pallas_call(
        functools.partial(_na_kernel, grid_rows=T // GRID_W),
        grid=(HEAD_PAIRS_NA // NA_PAIRS_PER_STEP, bsz, nb),
        in_specs=[
            pl.BlockSpec(blk, q_blk),
            pl.BlockSpec(blk, first_blk(kc)),
            pl.BlockSpec(blk, next_blk(kc)),
            pl.BlockSpec(blk, first_blk(vc)),
            pl.BlockSpec(blk, next_blk(vc)),
            pl.BlockSpec((CHUNK, step_lanes), lambda p, b, j: (0, kc + p)),
            pl.BlockSpec((CHUNK, step_lanes), lambda p, b, j: (0, vc + p)),
            pl.BlockSpec((NA_PAIRS_PER_STEP, 2, 2 * WIN_H - 2, GRID_W, two_heads), lambda p, b, j: (p, 0, 0, 0, 0)),
            pl.BlockSpec((NA_PAIRS_PER_STEP, two_heads, LANES), lambda p, b, j: (p, 0, 0)),
        ],
        out_specs=pl.BlockSpec(blk, lambda p, b, j: (b * nb + j, p)),
        out_shape=jax.ShapeDtypeStruct((bsz * T, D_NA), BF16),
        scratch_shapes=[
            pltpu.VMEM((3 * NA_BLOCK, step_lanes), BF16),
            pltpu.VMEM((3 * NA_BLOCK, step_lanes), BF16),
            pltpu.VMEM((units, two_heads, n_keys), F32),
            pltpu.VMEM((units, two_heads, n_keys), BF16),
            pltpu.VMEM((units, two_heads, 1), F32),
        ],
        compiler_params=_cparams(("arbitrary", "arbitrary", "arbitrary")),
        name="natten",
    )(qk_tok, qk_tok, qk_tok, proj_tok, proj_tok, qk_meta, proj_meta, tb, mb2)


def _na_meta(qk_meta, proj_meta, mb):
    qc, kc, vc = QK_COL_Q // LANES, QK_COL_K // LANES, COL_V // LANES
    blk = (CHUNK, LANES)
    return pl.pallas_call(
        _na_meta_kernel,
        grid=(HEAD_PAIRS_NA,),
        in_specs=[
            pl.BlockSpec(blk, lambda p: (0, qc + p)),
            pl.BlockSpec(blk, lambda p: (0, kc + p)),
            pl.BlockSpec(blk, lambda p: (0, vc + p)),
            pl.BlockSpec((1, 8, LANES), lambda p: (p, 0, 0)),
        ],
        out_specs=pl.BlockSpec(blk, lambda p: (0, p)),
        out_shape=jax.ShapeDtypeStruct((CHUNK, D_NA), BF16),
        compiler_params=_cparams(("arbitrary",)),
        name="natten_meta",
    )(qk_meta, qk_meta, proj_meta, mb)


def _rel_bias_table(rel_bias):
    w = np.arange(GRID_W)
    cstart = np.clip(w - WIN_W // 2, 0, GRID_W - WIN_W)
    cc = np.arange(GRID_W)
    inside = (cc[None, :] >= cstart[:, None]) & (cc[None, :] < cstart[:, None] + WIN_W)
    t = np.arange(2 * WIN_W - 1)
    onehot = ((cc[None, None, :] - w[None, :, None] + (WIN_W - 1)) == t[:, None, None]) & inside[None]
    per_row = jnp.einsum("hrt,twc->hrwc", rel_bias, jnp.asarray(onehot, F32), precision=lax.Precision.HIGHEST)
    per_row = jnp.where(inside[None, None], per_row, NEG_BIG)
    pairs = jnp.concatenate([per_row[:, :-1], per_row[:, 1:]], axis=-1)
    return pairs.reshape(HEAD_PAIRS_NA, 2, 2 * WIN_H - 2, GRID_W, 2 * GRID_W)


def _merge_kernel(ys_ref, yn_ref, gs_ref, gn_ref, h_ref, wbs_ref, wbn_ref, wo_ref, out_ref):
    m_s = jnp.dot(ys_ref[...], wbs_ref[...], preferred_element_type=F32)
    m_n = jnp.dot(yn_ref[...], wbn_ref[...], preferred_element_type=F32)
    merged = _sigmoid(gs_ref[...].astype(F32)) * m_s + _sigmoid(gn_ref[...].astype(F32)) * m_n
    out_ref[...] = h_ref[...] + jnp.dot(merged.astype(BF16), wo_ref[...], preferred_element_type=F32)


def _merge(y_ssd, y_na, proj, h, wbs, wbn, wo, tm, shared_rows):
    rows = y_ssd.shape[0]
    gsc, gnc = COL_GS // D_MODEL, COL_GN // D_MODEL
    r = (lambda i: 0) if shared_rows else (lambda i: i)
    full = lambda i: (0, 0)
    return pl.pallas_call(
        _merge_kernel,
        grid=(rows // tm,),
        in_specs=[
            pl.BlockSpec((tm, D_INNER), lambda i: (i, 0)),
            pl.BlockSpec((tm, D_NA), lambda i: (r(i), 0)),
            pl.BlockSpec((tm, D_MODEL), lambda i: (r(i), gsc)),
            pl.BlockSpec((tm, D_MODEL), lambda i: (r(i), gnc)),
            pl.BlockSpec((tm, D_MODEL), lambda i: (r(i), 0)),
            pl.BlockSpec((D_INNER, D_MODEL), full),
            pl.BlockSpec((D_NA, D_MODEL), full),
            pl.BlockSpec((D_MODEL, D_MODEL), full),
        ],
        out_specs=pl.BlockSpec((tm, D_MODEL), lambda i: (i, 0)),
        out_shape=jax.ShapeDtypeStruct((rows, D_MODEL), F32),
        compiler_params=_cparams(("parallel",)),
        name="merge",
    )(y_ssd, y_na, proj, proj, h, wbs, wbn, wo)


FFN_HALO = 16
FFN_TF = 256


def _ffn_kernel(h_ref, prev_ref, next_ref, meta_ref, g_ref, wup_ref, cw_ref, cb_ref, wd_ref, out_ref,
                u_scr, act_scr, *, tiles_per_seq, tm):
    pos = pl.program_id(0) % tiles_per_seq
    gain = g_ref[...]
    prev = jnp.where(pos == 0, meta_ref[...], prev_ref[...])
    nxt = jnp.where(pos == tiles_per_seq - 1, 0.0, next_ref[...])
    u_scr[0:FFN_HALO, :] = _rmsnorm_rows(prev, gain).astype(BF16)
    u_scr[FFN_HALO:FFN_HALO + tm, :] = _rmsnorm_rows(h_ref[...], gain).astype(BF16)
    u_scr[FFN_HALO + tm:, :] = _rmsnorm_rows(nxt, gain).astype(BF16)
    u = u_scr[...]

    n_ext = tm + 2 * FFN_HALO

    def conv(x_ext, cols):
        y = cw_ref[1:2, cols] * x_ext + cb_ref[:, cols]
        y = y + pltpu.roll(cw_ref[0:1, cols] * x_ext, 1, 0)
        y = y + pltpu.roll(cw_ref[2:3, cols] * x_ext, n_ext - 1, 0)
        return y[FFN_HALO:FFN_HALO + tm]

    for k in range(D_FF // FFN_TF):
        cols_a = slice(k * FFN_TF, (k + 1) * FFN_TF)
        cols_g = slice(D_FF + k * FFN_TF, D_FF + (k + 1) * FFN_TF)
        a = conv(jnp.dot(u, wup_ref[:, cols_a], preferred_element_type=F32), cols_a)
        g = conv(jnp.dot(u, wup_ref[:, cols_g], preferred_element_type=F32), cols_g)
        act_scr[:, cols_a] = (g * _sigmoid(g) * a).astype(BF16)
    out_ref[...] = h_ref[...] + jnp.dot(act_scr[...], wd_ref[...], preferred_element_type=F32)


def _ffn(h1, h1_meta, g_ffn, w_up, cw, cb, w_down, bsz, T, tm):
    rows = bsz * T
    tps = T // tm
    hb = tm // FFN_HALO
    mb = CHUNK // FFN_HALO
    nblk = rows // FFN_HALO
    const = lambda i: (0, 0)
    resident = pl.Buffered(1)
    return pl.pallas_call(
        functools.partial(_ffn_kernel, tiles_per_seq=tps, tm=tm),
        grid=(rows // tm,),
        in_specs=[
            pl.BlockSpec((tm, D_MODEL), lambda i: (i, 0)),
            pl.BlockSpec((FFN_HALO, D_MODEL), lambda i: (jnp.maximum(i * hb - 1, 0), 0)),
            pl.BlockSpec((FFN_HALO, D_MODEL), lambda i: (jnp.minimum((i + 1) * hb, nblk - 1), 0)),
            pl.BlockSpec((FFN_HALO, D_MODEL), lambda i: ((i // tps) * mb + mb - 1, 0)),
            pl.BlockSpec((1, D_MODEL), const),
            pl.BlockSpec((D_MODEL, 2 * D_FF), const, pipeline_mode=resident),
            pl.BlockSpec((8, 2 * D_FF), const),
            pl.BlockSpec((1, 2 * D_FF), const),
            pl.BlockSpec((D_FF, D_MODEL), const, pipeline_mode=resident),
        ],
        out_specs=pl.BlockSpec((tm, D_MODEL), lambda i: (i, 0)),
        out_shape=jax.ShapeDtypeStruct((rows, D_MODEL), F32),
        scratch_shapes=[
            pltpu.VMEM((tm + 2 * FFN_HALO, D_MODEL), BF16),
            pltpu.VMEM((tm, D_FF), BF16),
        ],
        compiler_params=_cparams(("parallel",)),
        name="ffn",
    )(h1, h1, h1, h1_meta, g_ffn, w_up, cw, cb, w_down)


def _pad_rows(a, n):
    return jnp.pad(a, ((0, n - a.shape[0]), (0, 0)))


def _row_tile(rows, cap):
    tm = min(cap, rows)
    assert rows % tm == 0
    return tm


def kernel(x_prompt, x_sample, meta_tokens, g_mix, w_in, ssd_conv_w, ssd_conv_b, dt_bias_f, dt_bias_b,
           a_log_f, a_log_b, d_skip, ssd_norm_g, q_norm_g, k_norm_g, rel_bias, meta_bias, w_br_ssd,
           w_br_na, w_out, g_ffn, w_up, ffn_conv_w, ffn_conv_b, w_down):
    assert g_mix.shape[0] == 1, "single-layer block"
    w = w_in[0]
    o_z, o_xbc = 0, D_INNER
    o_dtf = o_xbc + D_XBC
    o_q = o_dtf + 2 * SSD_HEADS
    o_v = o_q + 2 * D_NA
    w_main = jnp.concatenate([w[:, o_xbc:o_dtf], w[:, o_z:o_xbc], w[:, o_v:]], axis=1).astype(BF16)
    w_qk = w[:, o_q:o_v].astype(BF16)
    w_dt = jnp.pad(w[:, o_dtf:o_q], ((0, 0), (0, LANES - 2 * SSD_HEADS))).astype(BF16)
    g_mix2 = g_mix.astype(F32)
    lane_pad = lambda v: jnp.pad(v, (0, LANES - v.shape[0]))[None, :].astype(F32)
    dtb = lane_pad(jnp.concatenate([dt_bias_f[0], dt_bias_b[0]]))
    alog = lane_pad(jnp.concatenate([a_log_f[0], a_log_b[0]]))
    dskip = jnp.repeat(d_skip[0].astype(F32), SSD_HEADDIM)[None, :]
    ng = ssd_norm_g.astype(F32)
    conv_w = _pad_rows(ssd_conv_w[0].astype(F32), 8)
    conv_b = ssd_conv_b.astype(F32)
    qg = jnp.tile(q_norm_g[0].astype(F32), 2)[None, :] * (NA_HEADDIM ** -0.5)
    kg = jnp.tile(k_norm_g[0].astype(F32), 2)[None, :]
    qk_gain = jnp.stack([qg, kg])
    tb = _rel_bias_table(rel_bias[0].astype(F32))
    mb = jnp.pad(meta_bias[0].astype(F32).reshape(HEAD_PAIRS_NA, 2, N_META), ((0, 0), (0, 6), (0, LANES - N_META)))
    mb2 = jnp.repeat(meta_bias[0].astype(F32).reshape(HEAD_PAIRS_NA, 2, N_META), GRID_W, axis=1)
    mb2 = jnp.pad(mb2, ((0, 0), (0, 0), (META_PAD, 0)), constant_values=NEG_BIG)
    wbs, wbn, wo = w_br_ssd[0].astype(BF16), w_br_na[0].astype(BF16), w_out[0].astype(BF16)
    w_up_b, w_down_b = w_up[0].astype(BF16), w_down[0].astype(BF16)
    ffn_cw = _pad_rows(ffn_conv_w[0].astype(F32), 8)
    ffn_cb = ffn_conv_b.astype(F32)

    h_meta = jnp.concatenate([jnp.zeros((META_PAD, D_MODEL), F32), meta_tokens.astype(F32)], axis=0)
    proj_meta, dt_meta, qk_meta = _inproj(h_meta, g_mix2, w_main, w_dt, w_qk, qk_gain, CHUNK)
    dk_meta = _dt_prep(dt_meta, dtb, alog, 1, True)
    y_na_meta = _na_meta(qk_meta, proj_meta, mb)

    def run(x):
        bsz, T, _ = x.shape
        assert T % NA_BLOCK == 0 and T % CHUNK == 0
        x2d = x.reshape(bsz * T, D_MODEL)
        tm = _row_tile(bsz * T, 1024)
        proj_tok, dt_tok, qk_tok = _inproj(x2d, g_mix2, w_main, w_dt, w_qk, qk_gain, _row_tile(bsz * T, 2048))
        dk_tok = _dt_prep(dt_tok, dtb, alog, DT_PREP_CHUNKS, False)
        y_ssd, y_ssd_meta = _ssd(dk_tok, dk_meta, proj_tok, proj_meta, conv_w, conv_b, dskip, ng, bsz, T)
        y_na = _na(qk_tok, proj_tok, qk_meta, proj_meta, tb, mb2, bsz, T)
        h1 = _merge(y_ssd, y_na, proj_tok, x2d, wbs, wbn, wo, tm, False)
        h1_meta = _merge(y_ssd_meta, y_na_meta, proj_meta, h_meta, wbs, wbn, wo, CHUNK, True)
        tf = _row_tile(T, 1024)
        out = _ffn(h1, h1_meta, g_ffn.astype(F32), w_up_b, ffn_cw, ffn_cb, w_down_b, bsz, T, tf)
        return out.reshape(bsz, T, D_MODEL)

    return (run(x_prompt), run(x_sample))
```
